```python
import math
import jax, jax.numpy as jnp
from jax import lax
import numpy as np

D_MODEL = 4096
BATCH = 2
SEQ = 4096
DEPTH = 1

D_MIX = D_MODEL
MLA_HEADS = 16
QK_NOPE_DIM = 128
QK_ROPE_DIM = 64
QK_HEAD_DIM = QK_NOPE_DIM + QK_ROPE_DIM
V_HEAD_DIM = 128
Q_LORA_RANK = 1024
KV_LORA_RANK = 512
ROPE_BASE = 10000.0
ATTN_BLOCK = 128
MLA_WIDTH = MLA_HEADS * V_HEAD_DIM
GMLP_HEADS = 16
GMLP_CHUNK = 128
GMLP_WIDTH = D_MIX - MLA_WIDTH
GMLP_HEAD_DIM = GMLP_WIDTH // GMLP_HEADS
IN_WIDTH = Q_LORA_RANK + KV_LORA_RANK + QK_ROPE_DIM + 2 * GMLP_WIDTH
PEER_HEADS = 8
PEER_NKEYS = 128
PEER_N_EXPERTS = PEER_NKEYS * PEER_NKEYS
PEER_QUERY_DIM = 256
PEER_HALF = PEER_QUERY_DIM // 2
PEER_TOPK = 16
PEER_BLOCK = 128
NORM_EPS = 1e-6

kernel_name = "hybrid_mla_gmlp_peer_adaln_block"


def rms_norm(x, g):
    xf = x.astype(jnp.float32)
    y = xf * lax.rsqrt(jnp.mean(xf * xf, axis=-1, keepdims=True) + NORM_EPS)
    return (y * g.astype(jnp.float32)).astype(x.dtype)


def layer_norm(x, g):
    xf = x.astype(jnp.float32)
    mu = jnp.mean(xf, axis=-1, keepdims=True)
    var = jnp.mean(jnp.square(xf - mu), axis=-1, keepdims=True)
    return ((xf - mu) * lax.rsqrt(var + NORM_EPS) * g.astype(jnp.float32)).astype(x.dtype)


def modulate(h, shift, scale):
    return h * (1.0 + scale[:, None, :]) + shift[:, None, :]


def rope(x, cos, sin):
    x1, x2 = jnp.split(x, 2, axis=-1)
    return jnp.concatenate([x1 * cos - x2 * sin, x2 * cos + x1 * sin], axis=-1)


def mla_attention(q_lat, kv_lat, k_pe_raw, g_q, w_uq, g_kv, w_ukv, cos, sin):
    B, S, _ = q_lat.shape
    q = (rms_norm(q_lat, g_q) @ w_uq).reshape(B, S, MLA_HEADS, QK_HEAD_DIM)
    q_nope = q[..., :QK_NOPE_DIM]
    q_pe = rope(q[..., QK_NOPE_DIM:], cos[:, :, None, :], sin[:, :, None, :])
    kv = (rms_norm(kv_lat, g_kv) @ w_ukv).reshape(B, S, MLA_HEADS, QK_NOPE_DIM + V_HEAD_DIM)
    k_nope = kv[..., :QK_NOPE_DIM]
    v = kv[..., QK_NOPE_DIM:]
    k_pe = rope(k_pe_raw, cos, sin)
    nb = S // ATTN_BLOCK
    scale = 1.0 / math.sqrt(QK_HEAD_DIM)
    k_idx = jnp.arange(S)

    def to_blocks(t):
        return jnp.moveaxis(t.reshape(B, nb, ATTN_BLOCK, *t.shape[2:]), 1, 0)

    def block(args):
        qn, qp, i = args
        s = (jnp.einsum('bqhd,bkhd->bhqk', qn, k_nope)
             + jnp.einsum('bqhd,bkd->bhqk', qp, k_pe)).astype(jnp.float32) * scale
        q_idx = i * ATTN_BLOCK + jnp.arange(ATTN_BLOCK)
        causal = q_idx[:, None] >= k_idx[None, :]
        p = jax.nn.softmax(jnp.where(causal, s, -jnp.inf), axis=-1)
        return jnp.einsum('bhqk,bkhd->bqhd', p.astype(v.dtype), v)

    o = lax.map(block, (to_blocks(q_nope), to_blocks(q_pe), jnp.arange(nb)))
    return jnp.moveaxis(o, 0, 1).reshape(B, S, MLA_WIDTH)


def gmlp_sgu(z, g_sgu, w_sgu, b_sgu):
    B, S, _ = z.shape
    u, v = jnp.split(z, 2, axis=-1)
    v = layer_norm(v, g_sgu).reshape(B, S // GMLP_CHUNK, GMLP_CHUNK, GMLP_HEADS, GMLP_HEAD_DIM)
    causal = jnp.tril(jnp.ones((GMLP_CHUNK, GMLP_CHUNK), dtype=w_sgu.dtype))
    w = w_sgu * causal[None]
    sv = jnp.einsum('hts,bnshd->bnthd', w, v) + jnp.transpose(b_sgu)[None, None, :, :, None]
    return u * sv.reshape(B, S, GMLP_WIDTH)


def peer(xf, w_pq, keys, expert_u, expert_v):
    T, D = xf.shape
    blocks = xf.reshape(T // PEER_BLOCK, PEER_BLOCK, D)

    def block(xb):
        q = (xb @ w_pq).reshape(PEER_BLOCK, PEER_HEADS, 2, PEER_HALF)
        s = jnp.einsum('thpk,hpnk->thpn', q, keys).astype(jnp.float32)
        sv, si = lax.top_k(s, PEER_TOPK)
        cand = (sv[:, :, 0, :, None] + sv[:, :, 1, None, :]).reshape(PEER_BLOCK, PEER_HEADS, PEER_TOPK * PEER_TOPK)
        cidx = (si[:, :, 0, :, None] * PEER_NKEYS + si[:, :, 1, None, :]).reshape(PEER_BLOCK, PEER_HEADS, PEER_TOPK * PEER_TOPK)
        top_s, top_p = lax.top_k(cand, PEER_TOPK)
        eidx = jnp.take_along_axis(cidx, top_p, axis=-1)
        g = jax.nn.softmax(top_s, axis=-1).astype(xb.dtype)
        u = expert_u[eidx]
        a = jax.nn.gelu(jnp.einsum('thkd,td->thk', u, xb), approximate=False)
        vv = expert_v[eidx]
        return jnp.einsum('thk,thkd->td', g * a, vv)

    return lax.map(block, blocks).reshape(T, D)


def setup_inputs(seed: int = 0) -> dict:
    key = jax.random.key(seed)
    ks = jax.random.split(key, 32)
    f32 = jnp.float32
    nrm = lambda k, shape, s: jax.random.normal(k, shape, f32) * s
    gain = lambda k, shape: 1.0 + 0.1 * jax.random.normal(k, shape, f32)
    L = DEPTH
    x = jax.random.normal(ks[0], (BATCH, SEQ, D_MODEL), f32)
    c = jax.random.normal(ks[1], (BATCH, D_MODEL), f32)
    offsets = jax.random.randint(ks[2], (BATCH, 1), 0, 1024, dtype=jnp.int32)
    positions = (offsets + jnp.arange(SEQ, dtype=jnp.int32)[None, :]).astype(jnp.int32)
    return {
        "x": x,
        "c": c,
        "positions": positions,
        "w_ada": nrm(ks[3], (L, D_MODEL, 6 * D_MODEL), 0.5 * D_MODEL ** -0.5),
        "b_ada": nrm(ks[4], (L, 6 * D_MODEL), 0.02),
        "g_norm_mix": gain(ks[5], (L, D_MODEL)),
        "w_in": nrm(ks[6], (L, D_MODEL, IN_WIDTH), D_MODEL ** -0.5),
        "g_q": gain(ks[7], (L, Q_LORA_RANK)),
        "w_uq": nrm(ks[8], (L, Q_LORA_RANK, MLA_HEADS * QK_HEAD_DIM), Q_LORA_RANK ** -0.5),
        "g_kv": gain(ks[9], (L, KV_LORA_RANK)),
        "w_ukv": nrm(ks[10], (L, KV_LORA_RANK, MLA_HEADS * (QK_NOPE_DIM + V_HEAD_DIM)), KV_LORA_RANK ** -0.5),
        "g_sgu": gain(ks[11], (L, GMLP_WIDTH)),
        "w_sgu": nrm(ks[12], (L, GMLP_HEADS, GMLP_CHUNK, GMLP_CHUNK), GMLP_CHUNK ** -0.5),
        "b_sgu": gain(ks[13], (L, GMLP_HEADS, GMLP_CHUNK)),
        "beta_mla": gain(ks[14], (L, MLA_WIDTH)),
        "beta_gmlp": gain(ks[15], (L, GMLP_WIDTH)),
        "w_out": nrm(ks[16], (L, D_MIX, D_MODEL), D_MIX ** -0.5),
        "g_norm_ffn": gain(ks[17], (L, D_MODEL)),
        "w_pq": nrm(ks[18], (L, D_MODEL, PEER_HEADS * PEER_QUERY_DIM), D_MODEL ** -0.5),
        "peer_keys": nrm(ks[19], (L, PEER_HEADS, 2, PEER_NKEYS, PEER_HALF), PEER_HALF ** -0.5),
        "expert_u": nrm(ks[20], (L, PEER_N_EXPERTS, D_MODEL), D_MODEL ** -0.5),
        "expert_v": nrm(ks[21], (L, PEER_N_EXPERTS, D_MODEL), PEER_TOPK ** -0.5),
        "w_ada_f": nrm(ks[22], (D_MODEL, 2 * D_MODEL), 0.5 * D_MODEL ** -0.5),
        "b_ada_f": nrm(ks[23], (2 * D_MODEL,), 0.02),
        "g_norm_f": gain(ks[24], (D_MODEL,)),
    }


def reference(x, c, positions, w_ada, b_ada, g_norm_mix, w_in, g_q, w_uq, g_kv, w_ukv,
              g_sgu, w_sgu, b_sgu, beta_mla, beta_gmlp, w_out, g_norm_ffn, w_pq, peer_keys,
              expert_u, expert_v, w_ada_f, b_ada_f, g_norm_f):
    B, S, D = x.shape
    inv_freq = ROPE_BASE ** (-jnp.arange(0, QK_ROPE_DIM, 2, dtype=jnp.float32) / QK_ROPE_DIM)
    ang = positions.astype(jnp.float32)[..., None] * inv_freq
    cos = jnp.cos(ang).astype(x.dtype)
    sin = jnp.sin(ang).astype(x.dtype)
    c_act = jax.nn.silu(c)
    splits = [Q_LORA_RANK, Q_LORA_RANK + KV_LORA_RANK, Q_LORA_RANK + KV_LORA_RANK + QK_ROPE_DIM]
    for l in range(DEPTH):
        mod = c_act @ w_ada[l] + b_ada[l]
        sh_a, sc_a, gt_a, sh_f, sc_f, gt_f = jnp.split(mod, 6, axis=-1)
        h = modulate(rms_norm(x, g_norm_mix[l]), sh_a, sc_a)
        z = h @ w_in[l]
        q_lat, kv_lat, k_pe_raw, z_g = jnp.split(z, splits, axis=-1)
        y_mla = mla_attention(q_lat, kv_lat, k_pe_raw, g_q[l], w_uq[l], g_kv[l], w_ukv[l], cos, sin)
        y_g = gmlp_sgu(jax.nn.gelu(z_g, approximate=False), g_sgu[l], w_sgu[l], b_sgu[l])
        y = jnp.concatenate([rms_norm(y_mla, beta_mla[l]), rms_norm(y_g, beta_gmlp[l])], axis=-1)
        x = x + gt_a[:, None, :] * (y @ w_out[l])
        h = modulate(rms_norm(x, g_norm_ffn[l]), sh_f, sc_f)
        f = peer(h.reshape(B * S, D), w_pq[l], peer_keys[l], expert_u[l], expert_v[l]).reshape(B, S, D)
        x = x + gt_f[:, None, :] * f
    sh, sc = jnp.split(c_act @ w_ada_f + b_ada_f, 2, axis=-1)
    return modulate(rms_norm(x, g_norm_f), sh, sc)
```

```python
import functools
import math

import jax
import jax.numpy as jnp
from jax import lax
from jax.experimental import pallas as pl
from jax.experimental.pallas import tpu as pltpu

F32 = jnp.float32
BF16 = jnp.bfloat16

LANES = 128
MLA_HEADS = 16
QK_NOPE_DIM = 128
QK_ROPE_DIM = 64
QK_HEAD_DIM = QK_NOPE_DIM + QK_ROPE_DIM
Q_HEAD_PAD = 2 * LANES
V_HEAD_DIM = 128
Q_LORA_RANK = 1024
KV_LORA_RANK = 512
ROPE_BASE = 10000.0
GMLP_HEADS = 16
GMLP_CHUNK = 128
GMLP_HEAD_DIM = 128
PEER_HEADS = 8
PEER_NKEYS = 128
PEER_HALF = 128
PEER_TOPK = 16
PEER_SLOTS = PEER_HEADS * PEER_TOPK
NORM_EPS = 1e-6
VMEM_LIMIT = 56 * 1024 * 1024

_NT = (((1,), (1,)), ((), ()))


def _dot(a, b):
    return jnp.dot(a, b, preferred_element_type=F32)


def _dot_nt(a, b):
    return lax.dot_general(a, b, _NT, preferred_element_type=F32)


def _params(*sem):
    return pltpu.CompilerParams(dimension_semantics=sem, vmem_limit_bytes=VMEM_LIMIT)


def _rms(x, g):
    return x * lax.rsqrt(jnp.mean(x * x, axis=-1, keepdims=True) + NORM_EPS) * g


def _gelu(x):
    return 0.5 * x * (1.0 + lax.erf(x * math.sqrt(0.5)))


def _ada_kernel(c_ref, w_ref, b_ref, o_ref):
    c = c_ref[...]
    c_act = (c * jax.nn.sigmoid(c)).astype(BF16)
    o_ref[...] = _dot(c_act, w_ref[...].astype(BF16)) + b_ref[...]


def _ada(c_pad, w, b):
    rows, d = c_pad.shape
    n = w.shape[1]
    tn = 512
    return pl.pallas_call(
        _ada_kernel,
        grid=(n // tn,),
        in_specs=[
            pl.BlockSpec((rows, d), lambda j: (0, 0)),
            pl.BlockSpec((d, tn), lambda j: (0, j)),
            pl.BlockSpec((1, tn), lambda j: (0, j)),
        ],
        out_specs=pl.BlockSpec((rows, tn), lambda j: (0, j)),
        out_shape=jax.ShapeDtypeStruct((rows, n), F32),
        compiler_params=_params("parallel"),
        name="ada",
    )(c_pad, w, b.reshape(1, n))


def _rms_mod_kernel(x_ref, g_ref, sh_ref, sc_ref, o_ref):
    y = _rms(x_ref[...], g_ref[...])
    o_ref[...] = (y * (1.0 + sc_ref[...]) + sh_ref[...]).astype(o_ref.dtype)


def _rms_mod(x, g, shift, scale, seq, out_dtype):
    t, d = x.shape
    tm = 256
    per_batch = seq // tm
    mod_spec = pl.BlockSpec((None, 1, d), lambda i: (i // per_batch, 0, 0))
    return pl.pallas_call(
        _rms_mod_kernel,
        grid=(t // tm,),
        in_specs=[
            pl.BlockSpec((tm, d), lambda i: (i, 0)),
            pl.BlockSpec((1, d), lambda i: (0, 0)),
            mod_spec,
            mod_spec,
        ],
        out_specs=pl.BlockSpec((tm, d), lambda i: (i, 0)),
        out_shape=jax.ShapeDtypeStruct((t, d), out_dtype),
        compiler_params=_params("parallel"),
        name="rms_mod",
    )(x, g.reshape(1, d), shift[:, None, :], scale[:, None, :])


def _rms_kernel(x_ref, g_ref, o_ref):
    o_ref[...] = _rms(x_ref[...], g_ref[...]).astype(o_ref.dtype)


def _rms_rows(x, g, out_dtype):
    t, d = x.shape
    tm = 512
    return pl.pallas_call(
        _rms_kernel,
        grid=(t // tm,),
        in_specs=[pl.BlockSpec((tm, d), lambda i: (i, 0)), pl.BlockSpec((1, d), lambda i: (0, 0))],
        out_specs=pl.BlockSpec((tm, d), lambda i: (i, 0)),
        out_shape=jax.ShapeDtypeStruct((t, d), out_dtype),
        compiler_params=_params("parallel"),
        name="rms_rows",
    )(x, g.reshape(1, d))


def _final_kernel(x_ref, f_ref, gt_ref, g_ref, sh_ref, sc_ref, o_ref):
    x = x_ref[...] + gt_ref[...] * f_ref[...]
    o_ref[...] = _rms(x, g_ref[...]) * (1.0 + sc_ref[...]) + sh_ref[...]


def _final(x, f, gate, g, shift, scale, seq):
    t, d = x.shape
    tm = 256
    per_batch = seq // tm
    row = pl.BlockSpec((tm, d), lambda i: (i, 0))
    mod_spec = pl.BlockSpec((None, 1, d), lambda i: (i // per_batch, 0, 0))
    return pl.pallas_call(
        _final_kernel,
        grid=(t // tm,),
        in_specs=[row, row, mod_spec, pl.BlockSpec((1, d), lambda i: (0, 0)), mod_spec, mod_spec],
        out_specs=row,
        out_shape=jax.ShapeDtypeStruct((t, d), F32),
        compiler_params=_params("parallel"),
        name="final_norm",
    )(x, f, gate[:, None, :], g.reshape(1, d), shift[:, None, :], scale[:, None, :])


def _mm_kernel(a_ref, w_ref, o_ref):
    o_ref[...] = _dot(a_ref[...], w_ref[...].astype(BF16)).astype(o_ref.dtype)


def _matmul(a, w, tm, tn, out_dtype, name):
    m, k = a.shape
    n = w.shape[1]
    return pl.pallas_call(
        _mm_kernel,
        grid=(m // tm, n // tn),
        in_specs=[pl.BlockSpec((tm, k), lambda i, j: (i, 0)), pl.BlockSpec((k, tn), lambda i, j: (0, j))],
        out_specs=pl.BlockSpec((tm, tn), lambda i, j: (i, j)),
        out_shape=jax.ShapeDtypeStruct((m, n), out_dtype),
        compiler_params=_params("parallel", "parallel"),
        name=name,
    )(a, w)


def _swap_rope_halves(x):
    lane = lax.broadcasted_iota(jnp.int32, x.shape, 1)
    half = QK_ROPE_DIM // 2
    return jnp.where(lane < half, pltpu.roll(x, LANES - half, 1), pltpu.roll(x, half, 1))


def _rope_tile(x, cos_t, sin_t):
    return x * cos_t + _swap_rope_halves(x) * sin_t


def _q_proj_kernel(a_ref, w_ref, cos_ref, sin_ref, o_ref, *, heads_per_block, scale):
    acc = _dot(a_ref[...], w_ref[...])
    cos_t = cos_ref[...]
    sin_t = sin_ref[...]
    for h in range(heads_per_block):
        base = h * Q_HEAD_PAD
        nope = acc[:, base:base + LANES]
        pe = _rope_tile(acc[:, base + LANES:base + Q_HEAD_PAD], cos_t, sin_t)
        o_ref[:, base:base + LANES] = (nope * scale).astype(o_ref.dtype)
        o_ref[:, base + LANES:base + Q_HEAD_PAD] = (pe * scale).astype(o_ref.dtype)


def _q_proj(qn, w_uq_pad, cos_t, sin_t):
    m, k = qn.shape
    n = w_uq_pad.shape[1]
    tm, tn = 1024, 1024
    kern = functools.partial(_q_proj_kernel, heads_per_block=tn // Q_HEAD_PAD, scale=1.0 / math.sqrt(QK_HEAD_DIM))
    return pl.pallas_call(
        kern,
        grid=(m // tm, n // tn),
        in_specs=[
            pl.BlockSpec((tm, k), lambda i, j: (i, 0)),
            pl.BlockSpec((k, tn), lambda i, j: (0, j)),
            pl.BlockSpec((tm, LANES), lambda i, j: (i, 0)),
            pl.BlockSpec((tm, LANES), lambda i, j: (i, 0)),
        ],
        out_specs=pl.BlockSpec((tm, tn), lambda i, j: (i, j)),
        out_shape=jax.ShapeDtypeStruct((m, n), BF16),
        compiler_params=_params("parallel", "parallel"),
        name="q_proj",
    )(qn, w_uq_pad, cos_t, sin_t)


def _out_proj_kernel(a1_ref, a2_ref, w_ref, x_ref, gt_ref, o_ref):
    k1 = a1_ref.shape[1]
    w = w_ref[...].astype(BF16)
    acc = _dot(a1_ref[...], w[:k1]) + _dot(a2_ref[...], w[k1:])
    o_ref[...] = x_ref[...] + gt_ref[...] * acc


def _out_proj(a1, a2, w, x, gate, seq):
    m, k1 = a1.shape
    k2 = a2.shape[1]
    n = w.shape[1]
    tm, tn = 1024, 512
    per_batch = seq // tm
    return pl.pallas_call(
        _out_proj_kernel,
        grid=(m // tm, n // tn),
        in_specs=[
            pl.BlockSpec((tm, k1), lambda i, j: (i, 0)),
            pl.BlockSpec((tm, k2), lambda i, j: (i, 0)),
            pl.BlockSpec((k1 + k2, tn), lambda i, j: (0, j)),
            pl.BlockSpec((tm, tn), lambda i, j: (i, j)),
            pl.BlockSpec((None, 1, tn), lambda i, j: (i // per_batch, 0, j)),
        ],
        out_specs=pl.BlockSpec((tm, tn), lambda i, j: (i, j)),
        out_shape=jax.ShapeDtypeStruct((m, n), F32),
        compiler_params=_params("parallel", "parallel"),
        name="out_proj",
    )(a1, a2, w, x, gate[:, None, :])


def _latent_prep_kernel(z_ref, pos_ref, gq_ref, gkv_ref, freq_ref, cmask_ref, smask_ref,
                        qn_ref, kvn_ref, kpe_ref, cos_ref, sin_ref):
    q_lat = z_ref[:, :Q_LORA_RANK]
    kv_lat = z_ref[:, Q_LORA_RANK:Q_LORA_RANK + KV_LORA_RANK]
    k_pe = z_ref[:, Q_LORA_RANK + KV_LORA_RANK:]
    qn_ref[...] = _rms(q_lat, gq_ref[...]).astype(qn_ref.dtype)
    kvn_ref[...] = _rms(kv_lat, gkv_ref[...]).astype(kvn_ref.dtype)
    ang = pos_ref[...].astype(F32) * freq_ref[...]
    cos_t = jnp.cos(ang) * cmask_ref[...]
    sin_t = jnp.sin(ang) * smask_ref[...]
    cos_ref[...] = cos_t
    sin_ref[...] = sin_t
    kpe_ref[...] = _rope_tile(k_pe, cos_t, sin_t).astype(kpe_ref.dtype)


def _latent_prep(z_a, pos, g_q, g_kv):
    t, width = z_a.shape
    tm = 512
    half = QK_ROPE_DIM // 2
    inv_freq = ROPE_BASE ** (-jnp.arange(0, QK_ROPE_DIM, 2, dtype=F32) / QK_ROPE_DIM)
    zeros = jnp.zeros((LANES - QK_ROPE_DIM,), F32)
    freq = jnp.concatenate([inv_freq, inv_freq, zeros]).reshape(1, LANES)
    cmask = jnp.concatenate([jnp.ones((QK_ROPE_DIM,), F32), zeros]).reshape(1, LANES)
    smask = jnp.concatenate([-jnp.ones((half,), F32), jnp.ones((half,), F32), zeros]).reshape(1, LANES)
    const = lambda n: pl.BlockSpec((1, n), lambda i: (0, 0))
    rows = lambda n: pl.BlockSpec((tm, n), lambda i: (i, 0))
    return pl.pallas_call(
        _latent_prep_kernel,
        grid=(t // tm,),
        in_specs=[rows(width), rows(1), const(Q_LORA_RANK), const(KV_LORA_RANK),
                  const(LANES), const(LANES), const(LANES)],
        out_specs=[rows(Q_LORA_RANK), rows(KV_LORA_RANK), rows(LANES), rows(LANES), rows(LANES)],
        out_shape=[
            jax.ShapeDtypeStruct((t, Q_LORA_RANK), BF16),
            jax.ShapeDtypeStruct((t, KV_LORA_RANK), BF16),
            jax.ShapeDtypeStruct((t, LANES), BF16),
            jax.ShapeDtypeStruct((t, LANES), F32),
            jax.ShapeDtypeStruct((t, LANES), F32),
        ],
        compiler_params=_params("parallel"),
        name="latent_prep",
    )(z_a, pos, g_q.reshape(1, -1), g_kv.reshape(1, -1), freq, cmask, smask)


def _attn_kernel(q_ref, kv_ref, kpe_ref, o_ref, *, tq):
    i = pl.program_id(2)
    q = q_ref[...]
    q_nope = q[:, :LANES]
    q_pe = q[:, LANES:]

    def scores(j):
        off = pl.multiple_of(j * tq, tq)
        k_nope = kv_ref[pl.ds(off, tq), :QK_NOPE_DIM]
        v = kv_ref[pl.ds(off, tq), QK_NOPE_DIM:]
        k_pe = kpe_ref[pl.ds(off, tq), :]
        return _dot_nt(q_nope, k_nope) + _dot_nt(q_pe, k_pe), v

    def update(carry, s, v):
        m, l, acc = carry
        m_new = jnp.maximum(m, jnp.max(s, axis=-1, keepdims=True))
        alpha = jnp.exp(m - m_new)
        p = jnp.exp(s - m_new)
        l = alpha * l + jnp.sum(p, axis=-1, keepdims=True)
        acc = alpha * acc + _dot(p.astype(BF16), v)
        return m_new, l, acc

    def body(j, carry):
        s, v = scores(j)
        return update(carry, s, v)

    init = (jnp.full((tq, 1), -jnp.inf, F32), jnp.zeros((tq, 1), F32), jnp.zeros((tq, V_HEAD_DIM), F32))
    carry = lax.fori_loop(0, i, body, init)
    s, v = scores(i)
    row = lax.broadcasted_iota(jnp.int32, s.shape, 0)
    col = lax.broadcasted_iota(jnp.int32, s.shape, 1)
    _, l, acc = update(carry, jnp.where(row >= col, s, -jnp.inf), v)
    o_ref[...] = acc / l


def _attention(q, kv, kpe, batch, seq):
    tq = 512
    nq = seq // tq
    kern = functools.partial(_attn_kernel, tq=tq)
    return pl.pallas_call(
        kern,
        grid=(batch, MLA_HEADS, nq),
        in_specs=[
            pl.BlockSpec((tq, Q_HEAD_PAD), lambda b, h, i: (b * nq + i, h)),
            pl.BlockSpec((seq, QK_NOPE_DIM + V_HEAD_DIM), lambda b, h, i: (b, h)),
            pl.BlockSpec((seq, LANES), lambda b, h, i: (b, 0)),
        ],
        out_specs=pl.BlockSpec((tq, V_HEAD_DIM), lambda b, h, i: (b * nq + i, h)),
        out_shape=jax.ShapeDtypeStruct((batch * seq, MLA_HEADS * V_HEAD_DIM), F32),
        compiler_params=_params("parallel", "parallel", "arbitrary"),
        name="mla_attention",
    )(q, kv, kpe)


def _gmlp_kernel(z_ref, gs_ref, w_ref, bt_ref, beta_ref, o_ref, y_ref):
    rows = z_ref.shape[0]
    width = GMLP_HEADS * GMLP_HEAD_DIM
    u = _gelu(z_ref[:, :width])
    v = _gelu(z_ref[:, width:])
    mu = jnp.mean(v, axis=-1, keepdims=True)
    var = jnp.mean(jnp.square(v - mu), axis=-1, keepdims=True)
    vn = ((v - mu) * lax.rsqrt(var + NORM_EPS) * gs_ref[...]).astype(BF16)
    t_idx = lax.broadcasted_iota(jnp.int32, (GMLP_CHUNK, GMLP_CHUNK), 0)
    s_idx = lax.broadcasted_iota(jnp.int32, (GMLP_CHUNK, GMLP_CHUNK), 1)
    causal = (t_idx >= s_idx).astype(F32)
    for h in range(GMLP_HEADS):
        w = (w_ref[h] * causal).astype(BF16)
        bias = bt_ref[:, h:h + 1]
        cols = slice(h * GMLP_HEAD_DIM, (h + 1) * GMLP_HEAD_DIM)
        for c in range(rows // GMLP_CHUNK):
            rs = slice(c * GMLP_CHUNK, (c + 1) * GMLP_CHUNK)
            sv = _dot(w, vn[rs, cols]) + bias
            y_ref[rs, cols] = u[rs, cols] * sv
    o_ref[...] = _rms(y_ref[...], beta_ref[...]).astype(o_ref.dtype)


def _gmlp(z_g, g_sgu, w_sgu, b_sgu, beta):
    t, two_w = z_g.shape
    width = two_w // 2
    tm = 256
    return pl.pallas_call(
        _gmlp_kernel,
        grid=(t // tm,),
        in_specs=[
            pl.BlockSpec((tm, two_w), lambda i: (i, 0)),
            pl.BlockSpec((1, width), lambda i: (0, 0)),
            pl.BlockSpec((GMLP_HEADS, GMLP_CHUNK, GMLP_CHUNK), lambda i: (0, 0, 0)),
            pl.BlockSpec((GMLP_CHUNK, GMLP_HEADS), lambda i: (0, 0)),
            pl.BlockSpec((1, width), lambda i: (0, 0)),
        ],
        out_specs=pl.BlockSpec((tm, width), lambda i: (i, 0)),
        out_shape=jax.ShapeDtypeStruct((t, width), BF16),
        scratch_shapes=[pltpu.VMEM((tm, width), F32)],
        compiler_params=_params("parallel"),
        name="gmlp_sgu",
    )(z_g, g_sgu.reshape(1, width), w_sgu, jnp.transpose(b_sgu), beta.reshape(1, width))


ROUTE_TOKENS = 128
SCATTER_GROUP = 16
STAGE_PITCH = 136


def _top_k_rows(x, k, payload=None):
    n = x.shape[0]
    row = lax.broadcasted_iota(jnp.int32, x.shape, 0)
    vals, picks = [], []
    for _ in range(k):
        m = jnp.max(x, axis=0, keepdims=True)
        first = jnp.min(jnp.where(x == m, row, n), axis=0, keepdims=True)
        sel = row == first
        vals.append(m)
        picks.append(first if payload is None else jnp.max(jnp.where(sel, payload, -1), axis=0, keepdims=True))
        x = jnp.where(sel, -jnp.inf, x)
    return jnp.concatenate(vals, axis=0), jnp.concatenate(picks, axis=0)


def _route_kernel(q_ref, keys_ref, w_ref, gate_ref, row_ref, col_ref, stage_ref):
    gates, experts = [], []
    for h in range(PEER_HEADS):
        sub_v, sub_i = [], []
        for p in range(2):
            q_hp = q_ref[:, (2 * h + p) * PEER_HALF:(2 * h + p + 1) * PEER_HALF]
            s = _dot_nt(keys_ref[h, p], q_hp)
            v, idx = _top_k_rows(s, PEER_TOPK)
            sub_v.append(v)
            sub_i.append(idx)
        cand = jnp.concatenate([sub_v[0][a:a + 1] + sub_v[1] for a in range(PEER_TOPK)], axis=0)
        cidx = jnp.concatenate([sub_i[0][a:a + 1] * PEER_NKEYS + sub_i[1] for a in range(PEER_TOPK)], axis=0)
        top_s, eidx = _top_k_rows(cand, PEER_TOPK, payload=cidx)
        e = jnp.exp(top_s - top_s[0:1])
        gates.append(e / jnp.sum(e, axis=0, keepdims=True))
        experts.append(eidx)
    gate_t = jnp.concatenate(gates, axis=0).T
    expert_t = pltpu.bitcast(pltpu.bitcast(jnp.concatenate(experts, axis=0), F32).T, jnp.int32)
    gate_ref[...] = gate_t
    row_ref[...] = lax.shift_right_logical(expert_t, PEER_NKEYS.bit_length() - 1)
    col_ref[...] = lax.bitwise_and(expert_t, PEER_NKEYS - 1)

    sub = lax.broadcasted_iota(jnp.int32, (PEER_NKEYS, PEER_SLOTS), 0)

    def group(gi, carry):
        t0 = pl.multiple_of(gi * SCATTER_GROUP, SCATTER_GROUP)
        for tl in range(SCATTER_GROUP):
            g = jnp.broadcast_to(gate_ref[pl.ds(t0 + tl, 1), :], sub.shape)
            r = jnp.broadcast_to(row_ref[pl.ds(t0 + tl, 1), :], sub.shape)
            c = jnp.broadcast_to(col_ref[pl.ds(t0 + tl, 1), :], sub.shape)
            a = jnp.where(r == sub, g, 0.0).astype(BF16)
            b = jnp.where(c == sub, 1.0, 0.0).astype(BF16)
            stage_ref[pl.ds(tl * STAGE_PITCH, PEER_NKEYS), :] = _dot_nt(a, b)
        half = SCATTER_GROUP // 2
        for i in range(PEER_NKEYS):
            lo = stage_ref[pl.ds(i, half, stride=STAGE_PITCH), :]
            hi = stage_ref[pl.ds(half * STAGE_PITCH + i, half, stride=STAGE_PITCH), :]
            w_ref[pl.ds(t0, SCATTER_GROUP), i * PEER_NKEYS:(i + 1) * PEER_NKEYS] = (
                jnp.concatenate([lo, hi], axis=0).astype(w_ref.dtype))
        return carry

    lax.fori_loop(0, ROUTE_TOKENS // SCATTER_GROUP, group, 0)


def _route(q, keys):
    t, width = q.shape
    n_exp = PEER_NKEYS * PEER_NKEYS
    tr = ROUTE_TOKENS
    return pl.pallas_call(
        _route_kernel,
        grid=(t // tr,),
        in_specs=[
            pl.BlockSpec((tr, width), lambda i: (i, 0)),
            pl.BlockSpec(keys.shape, lambda i: (0, 0, 0, 0)),
        ],
        out_specs=pl.BlockSpec((tr, n_exp), lambda i: (i, 0)),
        out_shape=jax.ShapeDtypeStruct((t, n_exp), BF16),
        scratch_shapes=[
            pltpu.VMEM((tr, PEER_SLOTS), F32),
            pltpu.VMEM((tr, PEER_SLOTS), jnp.int32),
            pltpu.VMEM((tr, PEER_SLOTS), jnp.int32),
            pltpu.VMEM((SCATTER_GROUP * STAGE_PITCH, PEER_NKEYS), F32),
        ],
        compiler_params=_params("parallel"),
        name="peer_route",
    )(q, keys)


def _expert_kernel(h_ref, gate_ref, u_ref, v_ref, o_ref):
    @pl.when(pl.program_id(1) == 0)
    def _():
        o_ref[...] = jnp.zeros_like(o_ref)

    a = _gelu(_dot_nt(h_ref[...], u_ref[...].astype(BF16)))
    p = (gate_ref[...].astype(F32) * a).astype(BF16)
    o_ref[...] += _dot(p, v_ref[...].astype(BF16))


def _experts(h, gate, expert_u, expert_v):
    t, d = h.shape
    n_exp = expert_u.shape[0]
    tb, ec = 512, 256
    return pl.pallas_call(
        _expert_kernel,
        grid=(t // tb, n_exp // ec),
        in_specs=[
            pl.BlockSpec((tb, d), lambda i, c: (i, 0)),
            pl.BlockSpec((tb, ec), lambda i, c: (i, c)),
            pl.BlockSpec((ec, d), lambda i, c: (c, 0)),
            pl.BlockSpec((ec, d), lambda i, c: (c, 0)),
        ],
        out_specs=pl.BlockSpec((tb, d), lambda i, c: (i, 0)),
        out_shape=jax.ShapeDtypeStruct((t, d), F32),
        compiler_params=_params("parallel", "arbitrary"),
        name="peer_experts",
    )(h, gate, expert_u, expert_v)


def kernel(x, c, positions, w_ada, b_ada, g_norm_mix, w_in, g_q, w_uq, g_kv, w_ukv, g_sgu, w_sgu, b_sgu, beta_mla, beta_gmlp, w_out, g_norm_ffn, w_pq, peer_keys, expert_u, expert_v, w_ada_f, b_ada_f, g_norm_f):
    batch, seq, d = x.shape
    t = batch * seq
    assert w_ada.shape[0] == 1, "single-layer trunk"
    xf = x.reshape(t, d)

    c_pad = jnp.pad(c, ((0, 16 - batch), (0, 0)))
    mod = _ada(c_pad, w_ada[0], b_ada[0])[:batch]
    sh_a, sc_a, gt_a, sh_f, sc_f, gt_f = jnp.split(mod, 6, axis=-1)
    mod_f = _ada(c_pad, w_ada_f, b_ada_f)[:batch]
    sh_o, sc_o = jnp.split(mod_f, 2, axis=-1)

    lat = Q_LORA_RANK + KV_LORA_RANK + QK_ROPE_DIM
    w_in_a = jnp.pad(w_in[0][:, :lat], ((0, 0), (0, LANES - QK_ROPE_DIM))).astype(BF16)
    w_in_g = w_in[0][:, lat:].astype(BF16)
    w_uq_pad = jnp.pad(w_uq[0].reshape(Q_LORA_RANK, MLA_HEADS, QK_HEAD_DIM),
                       ((0, 0), (0, 0), (0, Q_HEAD_PAD - QK_HEAD_DIM))).reshape(Q_LORA_RANK, MLA_HEADS * Q_HEAD_PAD).astype(BF16)

    h1 = _rms_mod(xf, g_norm_mix[0], sh_a, sc_a, seq, BF16)
    z_a = _matmul(h1, w_in_a, 512, w_in_a.shape[1], F32, "in_proj_latent")
    z_g = _matmul(h1, w_in_g, 1024, 1024, F32, "in_proj_gmlp")
    qn, kvn, kpe, cos_t, sin_t = _latent_prep(z_a, positions.reshape(t, 1), g_q[0], g_kv[0])
    q = _q_proj(qn, w_uq_pad, cos_t, sin_t)
    kv = _matmul(kvn, w_ukv[0], 1024, 1024, BF16, "kv_proj")
    y_mla = _attention(q, kv, kpe, batch, seq)
    y1 = _rms_rows(y_mla, beta_mla[0], BF16)
    y2 = _gmlp(z_g, g_sgu[0], w_sgu[0], b_sgu[0], beta_gmlp[0])
    x2 = _out_proj(y1, y2, w_out[0], xf, gt_a, seq)

    h2 = _rms_mod(x2, g_norm_ffn[0], sh_f, sc_f, seq, BF16)
    pq = _matmul(h2, w_pq[0], 1024, 512, BF16, "peer_query")
    gate = _route(pq, peer_keys[0].astype(BF16))
    f = _experts(h2, gate, expert_u[0], expert_v[0])
    out = _final(x2, f, gt_f, g_norm_f, sh_o, sc_o, seq)
    return out.reshape(batch, seq, d)
```

```python
import functools
import math

import jax
import jax.numpy as jnp
from jax import lax
from jax.experimental import pallas as pl
from jax.experimental.pallas import tpu as pltpu

F32 = jnp.float32
BF16 = jnp.bfloat16

LANES = 128
MLA_HEADS = 16
QK_NOPE_DIM = 128
QK_ROPE_DIM = 64
QK_HEAD_DIM = QK_NOPE_DIM + QK_ROPE_DIM
Q_HEAD_PAD = 2 * LANES
V_HEAD_DIM = 128
Q_LORA_RANK = 1024
KV_LORA_RANK = 512
ROPE_BASE = 10000.0
GMLP_HEADS = 16
GMLP_CHUNK = 128
GMLP_HEAD_DIM = 128
PEER_HEADS = 8
PEER_NKEYS = 128
PEER_HALF = 128
PEER_TOPK = 16
PEER_SLOTS = PEER_HEADS * PEER_TOPK
NORM_EPS = 1e-6
VMEM_LIMIT = 56 * 1024 * 1024

_NT = (((1,), (1,)), ((), ()))


def _dot(a, b):
    return jnp.dot(a, b, preferred_element_type=F32)


def _dot_nt(a, b):
    return lax.dot_general(a, b, _NT, preferred_element_type=F32)


def _params(*sem):
    return pltpu.CompilerParams(dimension_semantics=sem, vmem_limit_bytes=VMEM_LIMIT)


def _rms(x, g):
    return x * lax.rsqrt(jnp.mean(x * x, axis=-1, keepdims=True) + NORM_EPS) * g


def _gelu(x):
    return 0.5 * x * (1.0 + lax.erf(x * math.sqrt(0.5)))


def _ada_kernel(c_ref, w_ref, b_ref, o_ref):
    c = c_ref[...]
    c_act = (c * jax.nn.sigmoid(c)).astype(BF16)
    o_ref[...] = _dot(c_act, w_ref[...].astype(BF16)) + b_ref[...]


def _ada(c_pad, w, b):
    rows, d = c_pad.shape
    n = w.shape[1]
    tn = 512
    return pl.pallas_call(
        _ada_kernel,
        grid=(n // tn,),
        in_specs=[
            pl.BlockSpec((rows, d), lambda j: (0, 0)),
            pl.BlockSpec((d, tn), lambda j: (0, j)),
            pl.BlockSpec((1, tn), lambda j: (0, j)),
        ],
        out_specs=pl.BlockSpec((rows, tn), lambda j: (0, j)),
        out_shape=jax.ShapeDtypeStruct((rows, n), F32),
        compiler_params=_params("parallel"),
        name="ada",
    )(c_pad, w, b.reshape(1, n))


def _rms_mod_kernel(x_ref, g_ref, sh_ref, sc_ref, o_ref):
    y = _rms(x_ref[...], g_ref[...])
    o_ref[...] = (y * (1.0 + sc_ref[...]) + sh_ref[...]).astype(o_ref.dtype)


def _rms_mod(x, g, shift, scale, seq, out_dtype):
    t, d = x.shape
    tm = 256
    per_batch = seq // tm
    mod_spec = pl.BlockSpec((None, 1, d), lambda i: (i // per_batch, 0, 0))
    return pl.pallas_call(
        _rms_mod_kernel,
        grid=(t // tm,),
        in_specs=[
            pl.BlockSpec((tm, d), lambda i: (i, 0)),
            pl.BlockSpec((1, d), lambda i: (0, 0)),
            mod_spec,
            mod_spec,
        ],
        out_specs=pl.BlockSpec((tm, d), lambda i: (i, 0)),
        out_shape=jax.ShapeDtypeStruct((t, d), out_dtype),
        compiler_params=_params("parallel"),
        name="rms_mod",
    )(x, g.reshape(1, d), shift[:, None, :], scale[:, None, :])


def _rms_kernel(x_ref, g_ref, o_ref):
    o_ref[...] = _rms(x_ref[...], g_ref[...]).astype(o_ref.dtype)


def _rms_rows(x, g, out_dtype):
    t, d = x.shape
    tm = 512
    return pl.pallas_call(
        _rms_kernel,
        grid=(t // tm,),
        in_specs=[pl.BlockSpec((tm, d), lambda i: (i, 0)), pl.BlockSpec((1, d), lambda i: (0, 0))],
        out_specs=pl.BlockSpec((tm, d), lambda i: (i, 0)),
        out_shape=jax.ShapeDtypeStruct((t, d), out_dtype),
        compiler_params=_params("parallel"),
        name="rms_rows",
    )(x, g.reshape(1, d))


def _final_kernel(x_ref, f_ref, gt_ref, g_ref, sh_ref, sc_ref, o_ref):
    x = x_ref[...] + gt_ref[...] * f_ref[...]
    o_ref[...] = _rms(x, g_ref[...]) * (1.0 + sc_ref[...]) + sh_ref[...]


def _final(x, f, gate, g, shift, scale, seq):
    t, d = x.shape
    tm = 256
    per_batch = seq // tm
    row = pl.BlockSpec((tm, d), lambda i: (i, 0))
    mod_spec = pl.BlockSpec((None, 1, d), lambda i: (i // per_batch, 0, 0))
    return pl.pallas_call(
        _final_kernel,
        grid=(t // tm,),
        in_specs=[row, row, mod_spec, pl.BlockSpec((1, d), lambda i: (0, 0)), mod_spec, mod_spec],
        out_specs=row,
        out_shape=jax.ShapeDtypeStruct((t, d), F32),
        compiler_params=_params("parallel"),
        name="final_norm",
    )(x, f, gate[:, None, :], g.reshape(1, d), shift[:, None, :], scale[:, None, :])


def _mm_kernel(a_ref, w_ref, o_ref):
    o_ref[...] = _dot(a_ref[...], w_ref[...].astype(BF16)).astype(o_ref.dtype)


def _matmul(a, w, tm, tn, out_dtype, name):
    m, k = a.shape
    n = w.shape[1]
    return pl.pallas_call(
        _mm_kernel,
        grid=(m // tm, n // tn),
        in_specs=[pl.BlockSpec((tm, k), lambda i, j: (i, 0)), pl.BlockSpec((k, tn), lambda i, j: (0, j))],
        out_specs=pl.BlockSpec((tm, tn), lambda i, j: (i, j)),
        out_shape=jax.ShapeDtypeStruct((m, n), out_dtype),
        compiler_params=_params("parallel", "parallel"),
        name=name,
    )(a, w)


def _swap_rope_halves(x):
    lane = lax.broadcasted_iota(jnp.int32, x.shape, 1)
    half = QK_ROPE_DIM // 2
    return jnp.where(lane < half, pltpu.roll(x, LANES - half, 1), pltpu.roll(x, half, 1))


def _rope_tile(x, cos_t, sin_t):
    return x * cos_t + _swap_rope_halves(x) * sin_t


def _q_proj_kernel(a_ref, w_ref, cos_ref, sin_ref, o_ref, *, heads_per_block, scale):
    acc = _dot(a_ref[...], w_ref[...])
    cos_t = cos_ref[...]
    sin_t = sin_ref[...]
    for h in range(heads_per_block):
        base = h * Q_HEAD_PAD
        nope = acc[:, base:base + LANES]
        pe = _rope_tile(acc[:, base + LANES:base + Q_HEAD_PAD], cos_t, sin_t)
        o_ref[:, base:base + LANES] = (nope * scale).astype(o_ref.dtype)
        o_ref[:, base + LANES:base + Q_HEAD_PAD] = (pe * scale).astype(o_ref.dtype)


def _q_proj(qn, w_uq_pad, cos_t, sin_t):
    m, k = qn.shape
    n = w_uq_pad.shape[1]
    tm, tn = 1024, 1024
    kern = functools.partial(_q_proj_kernel, heads_per_block=tn // Q_HEAD_PAD, scale=1.0 / math.sqrt(QK_HEAD_DIM))
    return pl.pallas_call(
        kern,
        grid=(m // tm, n // tn),
        in_specs=[
            pl.BlockSpec((tm, k), lambda i, j: (i, 0)),
            pl.BlockSpec((k, tn), lambda i, j: (0, j)),
            pl.BlockSpec((tm, LANES), lambda i, j: (i, 0)),
            pl.BlockSpec((tm, LANES), lambda i, j: (i, 0)),
        ],
        out_specs=pl.BlockSpec((tm, tn), lambda i, j: (i, j)),
        out_shape=jax.ShapeDtypeStruct((m, n), BF16),
        compiler_params=_params("parallel", "parallel"),
        name="q_proj",
    )(qn, w_uq_pad, cos_t, sin_t)


def _out_proj_kernel(a1_ref, a2_ref, w_ref, x_ref, gt_ref, o_ref):
    k1 = a1_ref.shape[1]
    w = w_ref[...].astype(BF16)
    acc = _dot(a1_ref[...], w[:k1]) + _dot(a2_ref[...], w[k1:])
    o_ref[...] = x_ref[...] + gt_ref[...] * acc


def _out_proj(a1, a2, w, x, gate, seq):
    m, k1 = a1.shape
    k2 = a2.shape[1]
    n = w.shape[1]
    tm, tn = 1024, 512
    per_batch = seq // tm
    return pl.pallas_call(
        _out_proj_kernel,
        grid=(m // tm, n // tn),
        in_specs=[
            pl.BlockSpec((tm, k1), lambda i, j: (i, 0)),
            pl.BlockSpec((tm, k2), lambda i, j: (i, 0)),
            pl.BlockSpec((k1 + k2, tn), lambda i, j: (0, j)),
            pl.BlockSpec((tm, tn), lambda i, j: (i, j)),
            pl.BlockSpec((None, 1, tn), lambda i, j: (i // per_batch, 0, j)),
        ],
        out_specs=pl.BlockSpec((tm, tn), lambda i, j: (i, j)),
        out_shape=jax.ShapeDtypeStruct((m, n), F32),
        compiler_params=_params("parallel", "parallel"),
        name="out_proj",
    )(a1, a2, w, x, gate[:, None, :])


def _latent_prep_kernel(z_ref, pos_ref, gq_ref, gkv_ref, freq_ref, cmask_ref, smask_ref,
                        qn_ref, kvn_ref, kpe_ref, cos_ref, sin_ref):
    q_lat = z_ref[:, :Q_LORA_RANK]
    kv_lat = z_ref[:, Q_LORA_RANK:Q_LORA_RANK + KV_LORA_RANK]
    k_pe = z_ref[:, Q_LORA_RANK + KV_LORA_RANK:]
    qn_ref[...] = _rms(q_lat, gq_ref[...]).astype(qn_ref.dtype)
    kvn_ref[...] = _rms(kv_lat, gkv_ref[...]).astype(kvn_ref.dtype)
    ang = pos_ref[...].astype(F32) * freq_ref[...]
    cos_t = jnp.cos(ang) * cmask_ref[...]
    sin_t = jnp.sin(ang) * smask_ref[...]
    cos_ref[...] = cos_t
    sin_ref[...] = sin_t
    kpe_ref[...] = _rope_tile(k_pe, cos_t, sin_t).astype(kpe_ref.dtype)


def _latent_prep(z_a, pos, g_q, g_kv):
    t, width = z_a.shape
    tm = 512
    half = QK_ROPE_DIM // 2
    inv_freq = ROPE_BASE ** (-jnp.arange(0, QK_ROPE_DIM, 2, dtype=F32) / QK_ROPE_DIM)
    zeros = jnp.zeros((LANES - QK_ROPE_DIM,), F32)
    freq = jnp.concatenate([inv_freq, inv_freq, zeros]).reshape(1, LANES)
    cmask = jnp.concatenate([jnp.ones((QK_ROPE_DIM,), F32), zeros]).reshape(1, LANES)
    smask = jnp.concatenate([-jnp.ones((half,), F32), jnp.ones((half,), F32), zeros]).reshape(1, LANES)
    const = lambda n: pl.BlockSpec((1, n), lambda i: (0, 0))
    rows = lambda n: pl.BlockSpec((tm, n), lambda i: (i, 0))
    return pl.pallas_call(
        _latent_prep_kernel,
        grid=(t // tm,),
        in_specs=[rows(width), rows(1), const(Q_LORA_RANK), const(KV_LORA_RANK),
                  const(LANES), const(LANES), const(LANES)],
        out_specs=[rows(Q_LORA_RANK), rows(KV_LORA_RANK), rows(LANES), rows(LANES), rows(LANES)],
        out_shape=[
            jax.ShapeDtypeStruct((t, Q_LORA_RANK), BF16),
            jax.ShapeDtypeStruct((t, KV_LORA_RANK), BF16),
            jax.ShapeDtypeStruct((t, LANES), BF16),
            jax.ShapeDtypeStruct((t, LANES), F32),
            jax.ShapeDtypeStruct((t, LANES), F32),
        ],
        compiler_params=_params("parallel"),
        name="latent_prep",
    )(z_a, pos, g_q.reshape(1, -1), g_kv.reshape(1, -1), freq, cmask, smask)


ATTN_HEADS_PER_STEP = 2


def _attn_kernel(q_ref, kv_ref, kpe_ref, o_ref, k_ref, *, tq):
    i = pl.program_id(2)
    kv_w = QK_NOPE_DIM + V_HEAD_DIM

    @pl.when(i == 0)
    def _():
        for hh in range(ATTN_HEADS_PER_STEP):
            k_ref[hh, :, :QK_NOPE_DIM] = kv_ref[:, hh * kv_w:hh * kv_w + QK_NOPE_DIM]
            k_ref[hh, :, QK_NOPE_DIM:] = kpe_ref[...]

    qs = [q_ref[:, hh * Q_HEAD_PAD:(hh + 1) * Q_HEAD_PAD] for hh in range(ATTN_HEADS_PER_STEP)]

    def step(j, carry, hh, masked):
        off = pl.multiple_of(j * tq, tq)
        s = _dot_nt(qs[hh], k_ref[hh, pl.ds(off, tq), :])
        if masked:
            row = lax.broadcasted_iota(jnp.int32, s.shape, 0)
            col = lax.broadcasted_iota(jnp.int32, s.shape, 1)
            s = jnp.where(row >= col, s, -jnp.inf)
        v = kv_ref[pl.ds(off, tq), hh * kv_w + QK_NOPE_DIM:(hh + 1) * kv_w]
        m, l, acc = carry
        m_new = jnp.maximum(m, jnp.max(s, axis=-1, keepdims=True))
        alpha = jnp.exp(m - m_new)
        p = jnp.exp(s - m_new)
        l = alpha * l + jnp.sum(p, axis=-1, keepdims=True)
        acc = alpha * acc + _dot(p.astype(BF16), v)
        return m_new, l, acc

    def body(j, carries):
        return tuple(step(j, carries[hh], hh, False) for hh in range(ATTN_HEADS_PER_STEP))

    init = (jnp.full((tq, 1), -jnp.inf, F32), jnp.zeros((tq, 1), F32), jnp.zeros((tq, V_HEAD_DIM), F32))
    carries = lax.fori_loop(0, i, body, (init,) * ATTN_HEADS_PER_STEP)
    for hh in range(ATTN_HEADS_PER_STEP):
        _, l, acc = step(i, carries[hh], hh, True)
        o_ref[:, hh * V_HEAD_DIM:(hh + 1) * V_HEAD_DIM] = acc / l


def _attention(q, kv, kpe, batch, seq):
    tq = 512
    nq = seq // tq
    hps = ATTN_HEADS_PER_STEP
    kern = functools.partial(_attn_kernel, tq=tq)
    return pl.pallas_call(
        kern,
        grid=(batch, MLA_HEADS // hps, nq),
        in_specs=[
            pl.BlockSpec((tq, hps * Q_HEAD_PAD), lambda b, h, i: (b * nq + i, h)),
            pl.BlockSpec((seq, hps * (QK_NOPE_DIM + V_HEAD_DIM)), lambda b, h, i: (b, h)),
            pl.BlockSpec((seq, LANES), lambda b, h, i: (b, 0)),
        ],
        out_specs=pl.BlockSpec((tq, hps * V_HEAD_DIM), lambda b, h, i: (b * nq + i, h)),
        out_shape=jax.ShapeDtypeStruct((batch * seq, MLA_HEADS * V_HEAD_DIM), F32),
        scratch_shapes=[pltpu.VMEM((hps, seq, Q_HEAD_PAD), BF16)],
        compiler_params=_params("parallel", "parallel", "arbitrary"),
        name="mla_attention",
    )(q, kv, kpe)


def _gmlp_kernel(z_ref, gs_ref, w_ref, bt_ref, beta_ref, o_ref, y_ref):
    rows = z_ref.shape[0]
    width = GMLP_HEADS * GMLP_HEAD_DIM
    u = _gelu(z_ref[:, :width])
    v = _gelu(z_ref[:, width:])
    mu = jnp.mean(v, axis=-1, keepdims=True)
    var = jnp.mean(jnp.square(v - mu), axis=-1, keepdims=True)
    vn = ((v - mu) * lax.rsqrt(var + NORM_EPS) * gs_ref[...]).astype(BF16)
    t_idx = lax.broadcasted_iota(jnp.int32, (GMLP_CHUNK, GMLP_CHUNK), 0)
    s_idx = lax.broadcasted_iota(jnp.int32, (GMLP_CHUNK, GMLP_CHUNK), 1)
    causal = (t_idx >= s_idx).astype(F32)
    for h in range(GMLP_HEADS):
        w = (w_ref[h] * causal).astype(BF16)
        bias = bt_ref[:, h:h + 1]
        cols = slice(h * GMLP_HEAD_DIM, (h + 1) * GMLP_HEAD_DIM)
        for c in range(rows // GMLP_CHUNK):
            rs = slice(c * GMLP_CHUNK, (c + 1) * GMLP_CHUNK)
            sv = _dot(w, vn[rs, cols]) + bias
            y_ref[rs, cols] = u[rs, cols] * sv
    o_ref[...] = _rms(y_ref[...], beta_ref[...]).astype(o_ref.dtype)


def _gmlp(z_g, g_sgu, w_sgu, b_sgu, beta):
    t, two_w = z_g.shape
    width = two_w // 2
    tm = 256
    return pl.pallas_call(
        _gmlp_kernel,
        grid=(t // tm,),
        in_specs=[
            pl.BlockSpec((tm, two_w), lambda i: (i, 0)),
            pl.BlockSpec((1, width), lambda i: (0, 0)),
            pl.BlockSpec((GMLP_HEADS, GMLP_CHUNK, GMLP_CHUNK), lambda i: (0, 0, 0)),
            pl.BlockSpec((GMLP_CHUNK, GMLP_HEADS), lambda i: (0, 0)),
            pl.BlockSpec((1, width), lambda i: (0, 0)),
        ],
        out_specs=pl.BlockSpec((tm, width), lambda i: (i, 0)),
        out_shape=jax.ShapeDtypeStruct((t, width), BF16),
        scratch_shapes=[pltpu.VMEM((tm, width), F32)],
        compiler_params=_params("parallel"),
        name="gmlp_sgu",
    )(z_g, g_sgu.reshape(1, width), w_sgu, jnp.transpose(b_sgu), beta.reshape(1, width))


ROUTE_TOKENS = 128
SCATTER_GROUP = 16
STAGE_PITCH = 136


def _top_k_rows(x, k, payload=None):
    n = x.shape[0]
    row = lax.broadcasted_iota(jnp.int32, x.shape, 0).astype(F32)
    vals, picks = [], []
    for _ in range(k):
        m = jnp.max(x, axis=0, keepdims=True)
        first = jnp.min(jnp.where(x == m, row, float(n)), axis=0, keepdims=True)
        sel = row == first
        vals.append(m)
        picks.append(first if payload is None else jnp.max(jnp.where(sel, payload, -1.0), axis=0, keepdims=True))
        x = jnp.where(sel, -jnp.inf, x)
    return jnp.concatenate(vals, axis=0), jnp.concatenate(picks, axis=0)


def _pair_candidates(v0, v1, i0, i1):
    k = PEER_TOPK
    tile = 8
    b_idx = lax.broadcasted_iota(jnp.int32, (tile, v0.shape[1]), 0)
    sums = [v0[0:1] + v1]
    ids = [i0[0:1] * float(PEER_NKEYS) + i1]
    for a in range(1, tile):
        sums.append(jnp.where(b_idx < k // (a + 1), v0[a:a + 1] + v1[0:tile], -jnp.inf))
        ids.append(i0[a:a + 1] * float(PEER_NKEYS) + i1[0:tile])
    sums.append(v0[tile:k] + v1[0:1])
    ids.append(i0[tile:k] * float(PEER_NKEYS) + i1[0:1])
    return jnp.concatenate(sums, axis=0), jnp.concatenate(ids, axis=0)


def _route_kernel(q_ref, keys_ref, w_ref, gate_ref, row_ref, col_ref, stage_ref):
    gates, experts = [], []
    for h in range(PEER_HEADS):
        sub_v, sub_i = [], []
        for p in range(2):
            q_hp = q_ref[:, (2 * h + p) * PEER_HALF:(2 * h + p + 1) * PEER_HALF]
            s = _dot_nt(keys_ref[h, p], q_hp)
            v, idx = _top_k_rows(s, PEER_TOPK)
            sub_v.append(v)
            sub_i.append(idx)
        cand, cidx = _pair_candidates(sub_v[0], sub_v[1], sub_i[0], sub_i[1])
        top_s, eidx = _top_k_rows(cand, PEER_TOPK, payload=cidx)
        e = jnp.exp(top_s - top_s[0:1])
        gates.append(e / jnp.sum(e, axis=0, keepdims=True))
        experts.append(eidx)
    gate_t = jnp.concatenate(gates, axis=0).T
    expert_t = jnp.concatenate(experts, axis=0).T.astype(jnp.int32)
    gate_ref[...] = gate_t
    row_ref[...] = lax.shift_right_logical(expert_t, PEER_NKEYS.bit_length() - 1)
    col_ref[...] = lax.bitwise_and(expert_t, PEER_NKEYS - 1)

    sub = lax.broadcasted_iota(jnp.int32, (PEER_NKEYS, PEER_SLOTS), 0)

    def group(gi, carry):
        t0 = pl.multiple_of(gi * SCATTER_GROUP, SCATTER_GROUP)
        for tl in range(SCATTER_GROUP):
            g = jnp.broadcast_to(gate_ref[pl.ds(t0 + tl, 1), :], sub.shape)
            r = jnp.broadcast_to(row_ref[pl.ds(t0 + tl, 1), :], sub.shape)
            c = jnp.broadcast_to(col_ref[pl.ds(t0 + tl, 1), :], sub.shape)
            a = jnp.where(r == sub, g, 0.0).astype(BF16)
            b = jnp.where(c == sub, 1.0, 0.0).astype(BF16)
            stage_ref[pl.ds(tl * STAGE_PITCH, PEER_NKEYS), :] = _dot_nt(a, b)
        half = SCATTER_GROUP // 2
        for i in range(PEER_NKEYS):
            lo = stage_ref[pl.ds(i, half, stride=STAGE_PITCH), :]
            hi = stage_ref[pl.ds(half * STAGE_PITCH + i, half, stride=STAGE_PITCH), :]
            w_ref[pl.ds(t0, SCATTER_GROUP), i * PEER_NKEYS:(i + 1) * PEER_NKEYS] = (
                jnp.concatenate([lo, hi], axis=0).astype(w_ref.dtype))
        return carry

    lax.fori_loop(0, ROUTE_TOKENS // SCATTER_GROUP, group, 0)


def _route(q, keys):
    t, width = q.shape
    n_exp = PEER_NKEYS * PEER_NKEYS
    tr = ROUTE_TOKENS
    return pl.pallas_call(
        _route_kernel,
        grid=(t // tr,),
        in_specs=[
            pl.BlockSpec((tr, width), lambda i: (i, 0)),
            pl.BlockSpec(keys.shape, lambda i: (0, 0, 0, 0)),
        ],
        out_specs=pl.BlockSpec((tr, n_exp), lambda i: (i, 0)),
        out_shape=jax.ShapeDtypeStruct((t, n_exp), BF16),
        scratch_shapes=[
            pltpu.VMEM((tr, PEER_SLOTS), F32),
            pltpu.VMEM((tr, PEER_SLOTS), jnp.int32),
            pltpu.VMEM((tr, PEER_SLOTS), jnp.int32),
            pltpu.VMEM((SCATTER_GROUP * STAGE_PITCH, PEER_NKEYS), F32),
        ],
        compiler_params=_params("parallel"),
        name="peer_route",
    )(q, keys)


EXPERT_COLS = 512


def _expert_kernel(h_ref, gate_ref, u_ref, v_ref, o_ref, p_ref):
    c = pl.program_id(1)

    @pl.when(c == 0)
    def _():
        o_ref[...] = jnp.zeros_like(o_ref)
        p_ref[1] = jnp.zeros(p_ref.shape[1:], p_ref.dtype)

    p_prev = p_ref[(c + 1) % 2]
    for n in range(o_ref.shape[1] // EXPERT_COLS):
        cols = slice(n * EXPERT_COLS, (n + 1) * EXPERT_COLS)
        o_ref[:, cols] += _dot(p_prev, v_ref[:, cols].astype(BF16))
    a = _gelu(_dot_nt(h_ref[...], u_ref[...].astype(BF16)))
    p_ref[c % 2] = (gate_ref[...].astype(F32) * a).astype(BF16)


def _experts(h, gate, expert_u, expert_v):
    t, d = h.shape
    n_exp = expert_u.shape[0]
    tb, ec = 1024, 256
    nc = n_exp // ec
    once = pl.Buffered(1)
    return pl.pallas_call(
        _expert_kernel,
        grid=(t // tb, nc + 1),
        in_specs=[
            pl.BlockSpec((tb, d), lambda i, c: (i, 0), pipeline_mode=once),
            pl.BlockSpec((tb, ec), lambda i, c: (i, jnp.minimum(c, nc - 1))),
            pl.BlockSpec((ec, d), lambda i, c: (jnp.minimum(c, nc - 1), 0)),
            pl.BlockSpec((ec, d), lambda i, c: (jnp.maximum(c - 1, 0), 0)),
        ],
        out_specs=pl.BlockSpec((tb, d), lambda i, c: (i, 0), pipeline_mode=once),
        out_shape=jax.ShapeDtypeStruct((t, d), F32),
        scratch_shapes=[pltpu.VMEM((2, tb, ec), BF16)],
        compiler_params=_params("parallel", "arbitrary"),
        name="peer_experts",
    )(h, gate, expert_u, expert_v)


def kernel(x, c, positions, w_ada, b_ada, g_norm_mix, w_in, g_q, w_uq, g_kv, w_ukv, g_sgu, w_sgu, b_sgu, beta_mla, beta_gmlp, w_out, g_norm_ffn, w_pq, peer_keys, expert_u, expert_v, w_ada_f, b_ada_f, g_norm_f):
    batch, seq, d = x.shape
    t = batch * seq
    assert w_ada.shape[0] == 1, "single-layer trunk"
    xf = x.reshape(t, d)

    c_pad = jnp.pad(c, ((0, 16 - batch), (0, 0)))
    mod = _ada(c_pad, w_ada[0], b_ada[0])[:batch]
    sh_a, sc_a, gt_a, sh_f, sc_f, gt_f = jnp.split(mod, 6, axis=-1)
    mod_f = _ada(c_pad, w_ada_f, b_ada_f)[:batch]
    sh_o, sc_o = jnp.split(mod_f, 2, axis=-1)

    lat = Q_LORA_RANK + KV_LORA_RANK + QK_ROPE_DIM
    w_in_a = jnp.pad(w_in[0][:, :lat], ((0, 0), (0, LANES - QK_ROPE_DIM))).astype(BF16)
    w_in_g = w_in[0][:, lat:].astype(BF16)
    w_uq_pad = jnp.pad(w_uq[0].reshape(Q_LORA_RANK, MLA_HEADS, QK_HEAD_DIM),
                       ((0, 0), (0, 0), (0, Q_HEAD_PAD - QK_HEAD_DIM))).reshape(Q_LORA_RANK, MLA_HEADS * Q_HEAD_PAD).astype(BF16)

    h1 = _rms_mod(xf, g_norm_mix[0], sh_a, sc_a, seq, BF16)
    z_a = _matmul(h1, w_in_a, 512, w_in_a.shape[1], F32, "in_proj_latent")
    z_g = _matmul(h1, w_in_g, 1024, 1024, F32, "in_proj_gmlp")
    qn, kvn, kpe, cos_t, sin_t = _latent_prep(z_a, positions.reshape(t, 1), g_q[0], g_kv[0])
    q = _q_proj(qn, w_uq_pad, cos_t, sin_t)
    kv = _matmul(kvn, w_ukv[0], 1024, 1024, BF16, "kv_proj")
    y_mla = _attention(q, kv, kpe, batch, seq)
    y1 = _rms_rows(y_mla, beta_mla[0], BF16)
    y2 = _gmlp(z_g, g_sgu[0], w_sgu[0], b_sgu[0], beta_gmlp[0])
    x2 = _out_proj(y1, y2, w_out[0], xf, gt_a, seq)

    h2 = _rms_mod(x2, g_norm_ffn[0], sh_f, sc_f, seq, BF16)
    pq = _matmul(h2, w_pq[0], 1024, 512, BF16, "peer_query")
    gate = _route(pq, peer_keys[0].astype(BF16))
    f = _experts(h2, gate, expert_u[0], expert_v[0])
    out = _final(x2, f, gt_f, g_norm_f, sh_o, sc_o, seq)
    return out.reshape(batch, seq, d)
```

```python
import functools
import math

import jax
import jax.numpy as jnp
from jax import lax
from jax.experimental import pallas as pl
from jax.experimental.pallas import tpu as pltpu

F32 = jnp.float32
BF16 = jnp.bfloat16

LANES = 128
MLA_HEADS = 16
QK_NOPE_DIM = 128
QK_ROPE_DIM = 64
QK_HEAD_DIM = QK_NOPE_DIM + QK_ROPE_DIM
Q_HEAD_PAD = 2 * LANES
V_HEAD_DIM = 128
Q_LORA_RANK = 1024
KV_LORA_RANK = 512
ROPE_BASE = 10000.0
GMLP_HEADS = 16
GMLP_CHUNK = 128
GMLP_HEAD_DIM = 128
PEER_HEADS = 8
PEER_NKEYS = 128
PEER_HALF = 128
PEER_TOPK = 16
PEER_SLOTS = PEER_HEADS * PEER_TOPK
NORM_EPS = 1e-6
VMEM_LIMIT = 56 * 1024 * 1024

_NT = (((1,), (1,)), ((), ()))


def _dot(a, b):
    return jnp.dot(a, b, preferred_element_type=F32)


def _dot_nt(a, b):
    return lax.dot_general(a, b, _NT, preferred_element_type=F32)


def _params(*sem):
    return pltpu.CompilerParams(dimension_semantics=sem, vmem_limit_bytes=VMEM_LIMIT)


def _rms(x, g):
    return x * lax.rsqrt(jnp.mean(x * x, axis=-1, keepdims=True) + NORM_EPS) * g


def _gelu(x):
    return 0.5 * x * (1.0 + lax.erf(x * math.sqrt(0.5)))


def _ada_kernel(c_ref, w_ref, b_ref, o_ref):
    c = c_ref[...]
    c_act = (c * jax.nn.sigmoid(c)).astype(BF16)
    o_ref[...] = _dot(c_act, w_ref[...].astype(BF16)) + b_ref[...]


def _ada(c_pad, w, b):
    rows, d = c_pad.shape
    n = w.shape[1]
    tn = 512
    return pl.pallas_call(
        _ada_kernel,
        grid=(n // tn,),
        in_specs=[
            pl.BlockSpec((rows, d), lambda j: (0, 0)),
            pl.BlockSpec((d, tn), lambda j: (0, j)),
            pl.BlockSpec((1, tn), lambda j: (0, j)),
        ],
        out_specs=pl.BlockSpec((rows, tn), lambda j: (0, j)),
        out_shape=jax.ShapeDtypeStruct((rows, n), F32),
        compiler_params=_params("parallel"),
        name="ada",
    )(c_pad, w, b.reshape(1, n))


def _rms_mod_kernel(x_ref, g_ref, sh_ref, sc_ref, o_ref):
    y = _rms(x_ref[...], g_ref[...])
    o_ref[...] = (y * (1.0 + sc_ref[...]) + sh_ref[...]).astype(o_ref.dtype)


def _rms_mod(x, g, shift, scale, seq, out_dtype):
    t, d = x.shape
    tm = 256
    per_batch = seq // tm
    mod_spec = pl.BlockSpec((None, 1, d), lambda i: (i // per_batch, 0, 0))
    return pl.pallas_call(
        _rms_mod_kernel,
        grid=(t // tm,),
        in_specs=[
            pl.BlockSpec((tm, d), lambda i: (i, 0)),
            pl.BlockSpec((1, d), lambda i: (0, 0)),
            mod_spec,
            mod_spec,
        ],
        out_specs=pl.BlockSpec((tm, d), lambda i: (i, 0)),
        out_shape=jax.ShapeDtypeStruct((t, d), out_dtype),
        compiler_params=_params("parallel"),
        name="rms_mod",
    )(x, g.reshape(1, d), shift[:, None, :], scale[:, None, :])


def _rms_kernel(x_ref, g_ref, o_ref):
    o_ref[...] = _rms(x_ref[...], g_ref[...]).astype(o_ref.dtype)


def _rms_rows(x, g, out_dtype):
    t, d = x.shape
    tm = 512
    return pl.pallas_call(
        _rms_kernel,
        grid=(t // tm,),
        in_specs=[pl.BlockSpec((tm, d), lambda i: (i, 0)), pl.BlockSpec((1, d), lambda i: (0, 0))],
        out_specs=pl.BlockSpec((tm, d), lambda i: (i, 0)),
        out_shape=jax.ShapeDtypeStruct((t, d), out_dtype),
        compiler_params=_params("parallel"),
        name="rms_rows",
    )(x, g.reshape(1, d))


def _final_kernel(x_ref, f_ref, gt_ref, g_ref, sh_ref, sc_ref, o_ref):
    x = x_ref[...] + gt_ref[...] * f_ref[...]
    o_ref[...] = _rms(x, g_ref[...]) * (1.0 + sc_ref[...]) + sh_ref[...]


def _final(x, f, gate, g, shift, scale, seq):
    t, d = x.shape
    tm = 256
    per_batch = seq // tm
    row = pl.BlockSpec((tm, d), lambda i: (i, 0))
    mod_spec = pl.BlockSpec((None, 1, d), lambda i: (i // per_batch, 0, 0))
    return pl.pallas_call(
        _final_kernel,
        grid=(t // tm,),
        in_specs=[row, row, mod_spec, pl.BlockSpec((1, d), lambda i: (0, 0)), mod_spec, mod_spec],
        out_specs=row,
        out_shape=jax.ShapeDtypeStruct((t, d), F32),
        compiler_params=_params("parallel"),
        name="final_norm",
    )(x, f, gate[:, None, :], g.reshape(1, d), shift[:, None, :], scale[:, None, :])


def _mm_kernel(a_ref, w_ref, o_ref):
    o_ref[...] = _dot(a_ref[...], w_ref[...].astype(BF16)).astype(o_ref.dtype)


def _matmul(a, w, tm, tn, out_dtype, name):
    m, k = a.shape
    n = w.shape[1]
    return pl.pallas_call(
        _mm_kernel,
        grid=(m // tm, n // tn),
        in_specs=[pl.BlockSpec((tm, k), lambda i, j: (i, 0)), pl.BlockSpec((k, tn), lambda i, j: (0, j))],
        out_specs=pl.BlockSpec((tm, tn), lambda i, j: (i, j)),
        out_shape=jax.ShapeDtypeStruct((m, n), out_dtype),
        compiler_params=_params("parallel", "parallel"),
        name=name,
    )(a, w)


def _swap_rope_halves(x):
    lane = lax.broadcasted_iota(jnp.int32, x.shape, 1)
    half = QK_ROPE_DIM // 2
    return jnp.where(lane < half, pltpu.roll(x, LANES - half, 1), pltpu.roll(x, half, 1))


def _rope_tile(x, cos_t, sin_t):
    return x * cos_t + _swap_rope_halves(x) * sin_t


def _q_proj_kernel(a_ref, w_ref, cos_ref, sin_ref, o_ref, *, heads_per_block, scale):
    acc = _dot(a_ref[...], w_ref[...])
    cos_t = cos_ref[...]
    sin_t = sin_ref[...]
    for h in range(heads_per_block):
        base = h * Q_HEAD_PAD
        nope = acc[:, base:base + LANES]
        pe = _rope_tile(acc[:, base + LANES:base + Q_HEAD_PAD], cos_t, sin_t)
        o_ref[:, base:base + LANES] = (nope * scale).astype(o_ref.dtype)
        o_ref[:, base + LANES:base + Q_HEAD_PAD] = (pe * scale).astype(o_ref.dtype)


def _q_proj(qn, w_uq_pad, cos_t, sin_t):
    m, k = qn.shape
    n = w_uq_pad.shape[1]
    tm, tn = 1024, 1024
    kern = functools.partial(_q_proj_kernel, heads_per_block=tn // Q_HEAD_PAD, scale=1.0 / math.sqrt(QK_HEAD_DIM))
    return pl.pallas_call(
        kern,
        grid=(m // tm, n // tn),
        in_specs=[
            pl.BlockSpec((tm, k), lambda i, j: (i, 0)),
            pl.BlockSpec((k, tn), lambda i, j: (0, j)),
            pl.BlockSpec((tm, LANES), lambda i, j: (i, 0)),
            pl.BlockSpec((tm, LANES), lambda i, j: (i, 0)),
        ],
        out_specs=pl.BlockSpec((tm, tn), lambda i, j: (i, j)),
        out_shape=jax.ShapeDtypeStruct((m, n), BF16),
        compiler_params=_params("parallel", "parallel"),
        name="q_proj",
    )(qn, w_uq_pad, cos_t, sin_t)


def _out_proj_kernel(a1_ref, a2_ref, w_ref, x_ref, gt_ref, o_ref):
    k1 = a1_ref.shape[1]
    w = w_ref[...].astype(BF16)
    acc = _dot(a1_ref[...], w[:k1]) + _dot(a2_ref[...], w[k1:])
    o_ref[...] = x_ref[...] + gt_ref[...] * acc


def _out_proj(a1, a2, w, x, gate, seq):
    m, k1 = a1.shape
    k2 = a2.shape[1]
    n = w.shape[1]
    tm, tn = 1024, 512
    per_batch = seq // tm
    return pl.pallas_call(
        _out_proj_kernel,
        grid=(m // tm, n // tn),
        in_specs=[
            pl.BlockSpec((tm, k1), lambda i, j: (i, 0)),
            pl.BlockSpec((tm, k2), lambda i, j: (i, 0)),
            pl.BlockSpec((k1 + k2, tn), lambda i, j: (0, j)),
            pl.BlockSpec((tm, tn), lambda i, j: (i, j)),
            pl.BlockSpec((None, 1, tn), lambda i, j: (i // per_batch, 0, j)),
        ],
        out_specs=pl.BlockSpec((tm, tn), lambda i, j: (i, j)),
        out_shape=jax.ShapeDtypeStruct((m, n), F32),
        compiler_params=_params("parallel", "parallel"),
        name="out_proj",
    )(a1, a2, w, x, gate[:, None, :])


def _latent_prep_kernel(z_ref, pos_ref, gq_ref, gkv_ref, freq_ref, cmask_ref, smask_ref,
                        qn_ref, kvn_ref, kpe_ref, cos_ref, sin_ref):
    q_lat = z_ref[:, :Q_LORA_RANK]
    kv_lat = z_ref[:, Q_LORA_RANK:Q_LORA_RANK + KV_LORA_RANK]
    k_pe = z_ref[:, Q_LORA_RANK + KV_LORA_RANK:]
    qn_ref[...] = _rms(q_lat, gq_ref[...]).astype(qn_ref.dtype)
    kvn_ref[...] = _rms(kv_lat, gkv_ref[...]).astype(kvn_ref.dtype)
    ang = pos_ref[...].astype(F32) * freq_ref[...]
    cos_t = jnp.cos(ang) * cmask_ref[...]
    sin_t = jnp.sin(ang) * smask_ref[...]
    cos_ref[...] = cos_t
    sin_ref[...] = sin_t
    kpe_ref[...] = _rope_tile(k_pe, cos_t, sin_t).astype(kpe_ref.dtype)


def _latent_prep(z_a, pos, g_q, g_kv):
    t, width = z_a.shape
    tm = 512
    half = QK_ROPE_DIM // 2
    inv_freq = ROPE_BASE ** (-jnp.arange(0, QK_ROPE_DIM, 2, dtype=F32) / QK_ROPE_DIM)
    zeros = jnp.zeros((LANES - QK_ROPE_DIM,), F32)
    freq = jnp.concatenate([inv_freq, inv_freq, zeros]).reshape(1, LANES)
    cmask = jnp.concatenate([jnp.ones((QK_ROPE_DIM,), F32), zeros]).reshape(1, LANES)
    smask = jnp.concatenate([-jnp.ones((half,), F32), jnp.ones((half,), F32), zeros]).reshape(1, LANES)
    const = lambda n: pl.BlockSpec((1, n), lambda i: (0, 0))
    rows = lambda n: pl.BlockSpec((tm, n), lambda i: (i, 0))
    return pl.pallas_call(
        _latent_prep_kernel,
        grid=(t // tm,),
        in_specs=[rows(width), rows(1), const(Q_LORA_RANK), const(KV_LORA_RANK),
                  const(LANES), const(LANES), const(LANES)],
        out_specs=[rows(Q_LORA_RANK), rows(KV_LORA_RANK), rows(LANES), rows(LANES), rows(LANES)],
        out_shape=[
            jax.ShapeDtypeStruct((t, Q_LORA_RANK), BF16),
            jax.ShapeDtypeStruct((t, KV_LORA_RANK), BF16),
            jax.ShapeDtypeStruct((t, LANES), BF16),
            jax.ShapeDtypeStruct((t, LANES), F32),
            jax.ShapeDtypeStruct((t, LANES), F32),
        ],
        compiler_params=_params("parallel"),
        name="latent_prep",
    )(z_a, pos, g_q.reshape(1, -1), g_kv.reshape(1, -1), freq, cmask, smask)


ATTN_HEADS_PER_STEP = 2


def _attn_kernel(q_ref, kv_ref, kpe_ref, o_ref, k_ref, *, tq):
    i = pl.program_id(2)
    kv_w = QK_NOPE_DIM + V_HEAD_DIM

    @pl.when(i == 0)
    def _():
        for hh in range(ATTN_HEADS_PER_STEP):
            k_ref[hh, :, :QK_NOPE_DIM] = kv_ref[:, hh * kv_w:hh * kv_w + QK_NOPE_DIM]
            k_ref[hh, :, QK_NOPE_DIM:] = kpe_ref[...]

    qs = [q_ref[:, hh * Q_HEAD_PAD:(hh + 1) * Q_HEAD_PAD] for hh in range(ATTN_HEADS_PER_STEP)]

    def step(j, carry, hh, masked):
        off = pl.multiple_of(j * tq, tq)
        s = _dot_nt(qs[hh], k_ref[hh, pl.ds(off, tq), :])
        if masked:
            row = lax.broadcasted_iota(jnp.int32, s.shape, 0)
            col = lax.broadcasted_iota(jnp.int32, s.shape, 1)
            s = jnp.where(row >= col, s, -jnp.inf)
        v = kv_ref[pl.ds(off, tq), hh * kv_w + QK_NOPE_DIM:(hh + 1) * kv_w]
        m, l, acc = carry
        m_new = jnp.maximum(m, jnp.max(s, axis=-1, keepdims=True))
        alpha = jnp.exp(m - m_new)
        p = jnp.exp(s - m_new)
        l = alpha * l + jnp.sum(p, axis=-1, keepdims=True)
        acc = alpha * acc + _dot(p.astype(BF16), v)
        return m_new, l, acc

    def body(j, carries):
        return tuple(step(j, carries[hh], hh, False) for hh in range(ATTN_HEADS_PER_STEP))

    init = (jnp.full((tq, 1), -jnp.inf, F32), jnp.zeros((tq, 1), F32), jnp.zeros((tq, V_HEAD_DIM), F32))
    carries = lax.fori_loop(0, i, body, (init,) * ATTN_HEADS_PER_STEP)
    for hh in range(ATTN_HEADS_PER_STEP):
        _, l, acc = step(i, carries[hh], hh, True)
        o_ref[:, hh * V_HEAD_DIM:(hh + 1) * V_HEAD_DIM] = acc / l


def _attention(q, kv, kpe, batch, seq):
    tq = 512
    nq = seq // tq
    hps = ATTN_HEADS_PER_STEP
    kern = functools.partial(_attn_kernel, tq=tq)
    return pl.pallas_call(
        kern,
        grid=(batch, MLA_HEADS // hps, nq),
        in_specs=[
            pl.BlockSpec((tq, hps * Q_HEAD_PAD), lambda b, h, i: (b * nq + i, h)),
            pl.BlockSpec((seq, hps * (QK_NOPE_DIM + V_HEAD_DIM)), lambda b, h, i: (b, h)),
            pl.BlockSpec((seq, LANES), lambda b, h, i: (b, 0)),
        ],
        out_specs=pl.BlockSpec((tq, hps * V_HEAD_DIM), lambda b, h, i: (b * nq + i, h)),
        out_shape=jax.ShapeDtypeStruct((batch * seq, MLA_HEADS * V_HEAD_DIM), F32),
        scratch_shapes=[pltpu.VMEM((hps, seq, Q_HEAD_PAD), BF16)],
        compiler_params=_params("parallel", "parallel", "arbitrary"),
        name="mla_attention",
    )(q, kv, kpe)


def _gmlp_kernel(z_ref, gs_ref, w_ref, bt_ref, beta_ref, o_ref, y_ref):
    rows = z_ref.shape[0]
    width = GMLP_HEADS * GMLP_HEAD_DIM
    u = _gelu(z_ref[:, :width])
    v = _gelu(z_ref[:, width:])
    mu = jnp.mean(v, axis=-1, keepdims=True)
    var = jnp.mean(jnp.square(v - mu), axis=-1, keepdims=True)
    vn = ((v - mu) * lax.rsqrt(var + NORM_EPS) * gs_ref[...]).astype(BF16)
    t_idx = lax.broadcasted_iota(jnp.int32, (GMLP_CHUNK, GMLP_CHUNK), 0)
    s_idx = lax.broadcasted_iota(jnp.int32, (GMLP_CHUNK, GMLP_CHUNK), 1)
    causal = (t_idx >= s_idx).astype(F32)
    for h in range(GMLP_HEADS):
        w = (w_ref[h] * causal).astype(BF16)
        bias = bt_ref[:, h:h + 1]
        cols = slice(h * GMLP_HEAD_DIM, (h + 1) * GMLP_HEAD_DIM)
        for c in range(rows // GMLP_CHUNK):
            rs = slice(c * GMLP_CHUNK, (c + 1) * GMLP_CHUNK)
            sv = _dot(w, vn[rs, cols]) + bias
            y_ref[rs, cols] = u[rs, cols] * sv
    o_ref[...] = _rms(y_ref[...], beta_ref[...]).astype(o_ref.dtype)


def _gmlp(z_g, g_sgu, w_sgu, b_sgu, beta):
    t, two_w = z_g.shape
    width = two_w // 2
    tm = 256
    return pl.pallas_call(
        _gmlp_kernel,
        grid=(t // tm,),
        in_specs=[
            pl.BlockSpec((tm, two_w), lambda i: (i, 0)),
            pl.BlockSpec((1, width), lambda i: (0, 0)),
            pl.BlockSpec((GMLP_HEADS, GMLP_CHUNK, GMLP_CHUNK), lambda i: (0, 0, 0)),
            pl.BlockSpec((GMLP_CHUNK, GMLP_HEADS), lambda i: (0, 0)),
            pl.BlockSpec((1, width), lambda i: (0, 0)),
        ],
        out_specs=pl.BlockSpec((tm, width), lambda i: (i, 0)),
        out_shape=jax.ShapeDtypeStruct((t, width), BF16),
        scratch_shapes=[pltpu.VMEM((tm, width), F32)],
        compiler_params=_params("parallel"),
        name="gmlp_sgu",
    )(z_g, g_sgu.reshape(1, width), w_sgu, jnp.transpose(b_sgu), beta.reshape(1, width))


ROUTE_TOKENS = 128
SCATTER_GROUP = 16
STAGE_PITCH = 136


def _top_k_rows(x, k, payload=None):
    n = x.shape[0]
    row = lax.broadcasted_iota(jnp.int32, x.shape, 0).astype(F32)
    vals, picks = [], []
    for _ in range(k):
        m = jnp.max(x, axis=0, keepdims=True)
        first = jnp.min(jnp.where(x == m, row, float(n)), axis=0, keepdims=True)
        sel = row == first
        vals.append(m)
        picks.append(first if payload is None else jnp.max(jnp.where(sel, payload, -1.0), axis=0, keepdims=True))
        x = jnp.where(sel, -jnp.inf, x)
    return jnp.concatenate(vals, axis=0), jnp.concatenate(picks, axis=0)


def _pair_candidates(v0, v1, i0, i1):
    k = PEER_TOPK
    tile = 8
    b_idx = lax.broadcasted_iota(jnp.int32, (tile, v0.shape[1]), 0)
    sums = [v0[0:1] + v1]
    ids = [i0[0:1] * float(PEER_NKEYS) + i1]
    for a in range(1, tile):
        sums.append(jnp.where(b_idx < k // (a + 1), v0[a:a + 1] + v1[0:tile], -jnp.inf))
        ids.append(i0[a:a + 1] * float(PEER_NKEYS) + i1[0:tile])
    sums.append(v0[tile:k] + v1[0:1])
    ids.append(i0[tile:k] * float(PEER_NKEYS) + i1[0:1])
    return jnp.concatenate(sums, axis=0), jnp.concatenate(ids, axis=0)


def _route_kernel(q_ref, keys_ref, w_ref, gate_ref, row_ref, col_ref, stage_ref):
    gates, experts = [], []
    for h in range(PEER_HEADS):
        sub_v, sub_i = [], []
        for p in range(2):
            q_hp = q_ref[:, (2 * h + p) * PEER_HALF:(2 * h + p + 1) * PEER_HALF]
            s = _dot_nt(keys_ref[h, p], q_hp)
            v, idx = _top_k_rows(s, PEER_TOPK)
            sub_v.append(v)
            sub_i.append(idx)
        cand, cidx = _pair_candidates(sub_v[0], sub_v[1], sub_i[0], sub_i[1])
        top_s, eidx = _top_k_rows(cand, PEER_TOPK, payload=cidx)
        e = jnp.exp(top_s - top_s[0:1])
        gates.append(e / jnp.sum(e, axis=0, keepdims=True))
        experts.append(eidx)
    gate_t = jnp.concatenate(gates, axis=0).T
    expert_t = jnp.concatenate(experts, axis=0).T.astype(jnp.int32)
    gate_ref[...] = gate_t
    row_ref[...] = lax.shift_right_logical(expert_t, PEER_NKEYS.bit_length() - 1)
    col_ref[...] = lax.bitwise_and(expert_t, PEER_NKEYS - 1)

    sub = lax.broadcasted_iota(jnp.int32, (PEER_NKEYS, PEER_SLOTS), 0)

    def group(gi, carry):
        t0 = pl.multiple_of(gi * SCATTER_GROUP, SCATTER_GROUP)
        for tl in range(SCATTER_GROUP):
            g = jnp.broadcast_to(gate_ref[pl.ds(t0 + tl, 1), :], sub.shape)
            r = jnp.broadcast_to(row_ref[pl.ds(t0 + tl, 1), :], sub.shape)
            c = jnp.broadcast_to(col_ref[pl.ds(t0 + tl, 1), :], sub.shape)
            a = jnp.where(r == sub, g, 0.0).astype(BF16)
            b = jnp.where(c == sub, 1.0, 0.0).astype(BF16)
            stage_ref[pl.ds(tl * STAGE_PITCH, PEER_NKEYS), :] = _dot_nt(a, b)
        half = SCATTER_GROUP // 2
        for i in range(PEER_NKEYS):
            lo = stage_ref[pl.ds(i, half, stride=STAGE_PITCH), :]
            hi = stage_ref[pl.ds(half * STAGE_PITCH + i, half, stride=STAGE_PITCH), :]
            w_ref[pl.ds(t0, SCATTER_GROUP), i * PEER_NKEYS:(i + 1) * PEER_NKEYS] = (
                jnp.concatenate([lo, hi], axis=0).astype(w_ref.dtype))
        return carry

    lax.fori_loop(0, ROUTE_TOKENS // SCATTER_GROUP, group, 0)


def _route(q, keys):
    t, width = q.shape
    n_exp = PEER_NKEYS * PEER_NKEYS
    tr = ROUTE_TOKENS
    return pl.pallas_call(
        _route_kernel,
        grid=(t // tr,),
        in_specs=[
            pl.BlockSpec((tr, width), lambda i: (i, 0)),
            pl.BlockSpec(keys.shape, lambda i: (0, 0, 0, 0)),
        ],
        out_specs=pl.BlockSpec((tr, n_exp), lambda i: (i, 0)),
        out_shape=jax.ShapeDtypeStruct((t, n_exp), BF16),
        scratch_shapes=[
            pltpu.VMEM((tr, PEER_SLOTS), F32),
            pltpu.VMEM((tr, PEER_SLOTS), jnp.int32),
            pltpu.VMEM((tr, PEER_SLOTS), jnp.int32),
            pltpu.VMEM((SCATTER_GROUP * STAGE_PITCH, PEER_NKEYS), F32),
        ],
        compiler_params=_params("parallel"),
        name="peer_route",
    )(q, keys)


EXPERT_COLS = 512


def _expert_kernel(h_ref, gate_ref, u_ref, v_ref, o_ref, p_ref):
    c = pl.program_id(1)

    @pl.when(c == 0)
    def _():
        o_ref[...] = jnp.zeros_like(o_ref)
        p_ref[1] = jnp.zeros(p_ref.shape[1:], p_ref.dtype)

    p_prev = p_ref[(c + 1) % 2]
    for n in range(o_ref.shape[1] // EXPERT_COLS):
        cols = slice(n * EXPERT_COLS, (n + 1) * EXPERT_COLS)
        o_ref[:, cols] += _dot(p_prev, v_ref[:, cols])
    a = _gelu(_dot_nt(h_ref[...], u_ref[...]))
    p_ref[c % 2] = (gate_ref[...].astype(F32) * a).astype(BF16)


def _experts(h, gate, expert_u, expert_v):
    t, d = h.shape
    n_exp = expert_u.shape[0]
    tb, ec = 1024, 512
    nc = n_exp // ec
    once = pl.Buffered(1)
    return pl.pallas_call(
        _expert_kernel,
        grid=(t // tb, nc + 1),
        in_specs=[
            pl.BlockSpec((tb, d), lambda i, c: (i, 0), pipeline_mode=once),
            pl.BlockSpec((tb, ec), lambda i, c: (i, jnp.minimum(c, nc - 1))),
            pl.BlockSpec((ec, d), lambda i, c: (jnp.minimum(c, nc - 1), 0)),
            pl.BlockSpec((ec, d), lambda i, c: (jnp.maximum(c - 1, 0), 0)),
        ],
        out_specs=pl.BlockSpec((tb, d), lambda i, c: (i, 0), pipeline_mode=once),
        out_shape=jax.ShapeDtypeStruct((t, d), F32),
        scratch_shapes=[pltpu.VMEM((2, tb, ec), BF16)],
        compiler_params=_params("parallel", "arbitrary"),
        name="peer_experts",
    )(h, gate, expert_u, expert_v)


def kernel(x, c, positions, w_ada, b_ada, g_norm_mix, w_in, g_q, w_uq, g_kv, w_ukv, g_sgu, w_sgu, b_sgu, beta_mla, beta_gmlp, w_out, g_norm_ffn, w_pq, peer_keys, expert_u, expert_v, w_ada_f, b_ada_f, g_norm_f):
    batch, seq, d = x.shape
    t = batch * seq
    assert w_ada.shape[0] == 1, "single-layer trunk"
    xf = x.reshape(t, d)

    c_pad = jnp.pad(c, ((0, 16 - batch), (0, 0)))
    mod = _ada(c_pad, w_ada[0], b_ada[0])[:batch]
    sh_a, sc_a, gt_a, sh_f, sc_f, gt_f = jnp.split(mod, 6, axis=-1)
    mod_f = _ada(c_pad, w_ada_f, b_ada_f)[:batch]
    sh_o, sc_o = jnp.split(mod_f, 2, axis=-1)

    lat = Q_LORA_RANK + KV_LORA_RANK + QK_ROPE_DIM
    w_in_a = jnp.pad(w_in[0][:, :lat], ((0, 0), (0, LANES - QK_ROPE_DIM))).astype(BF16)
    w_in_g = w_in[0][:, lat:].astype(BF16)
    w_uq_pad = jnp.pad(w_uq[0].reshape(Q_LORA_RANK, MLA_HEADS, QK_HEAD_DIM),
                       ((0, 0), (0, 0), (0, Q_HEAD_PAD - QK_HEAD_DIM))).reshape(Q_LORA_RANK, MLA_HEADS * Q_HEAD_PAD).astype(BF16)

    h1 = _rms_mod(xf, g_norm_mix[0], sh_a, sc_a, seq, BF16)
    z_a = _matmul(h1, w_in_a, 512, w_in_a.shape[1], F32, "in_proj_latent")
    z_g = _matmul(h1, w_in_g, 1024, 1024, F32, "in_proj_gmlp")
    qn, kvn, kpe, cos_t, sin_t = _latent_prep(z_a, positions.reshape(t, 1), g_q[0], g_kv[0])
    q = _q_proj(qn, w_uq_pad, cos_t, sin_t)
    kv = _matmul(kvn, w_ukv[0], 1024, 1024, BF16, "kv_proj")
    y_mla = _attention(q, kv, kpe, batch, seq)
    y1 = _rms_rows(y_mla, beta_mla[0], BF16)
    y2 = _gmlp(z_g, g_sgu[0], w_sgu[0], b_sgu[0], beta_gmlp[0])
    x2 = _out_proj(y1, y2, w_out[0], xf, gt_a, seq)

    h2 = _rms_mod(x2, g_norm_ffn[0], sh_f, sc_f, seq, BF16)
    pq = _matmul(h2, w_pq[0], 1024, 512, BF16, "peer_query")
    gate = _route(pq, peer_keys[0].astype(BF16))
    f = _experts(h2, gate, expert_u[0].astype(BF16), expert_v[0].astype(BF16))
    out = _final(x2, f, gt_f, g_norm_f, sh_o, sc_o, seq)
    return out.reshape(batch, seq, d)
```

```python
import functools
import math

import jax
import jax.numpy as jnp
from jax import lax
from jax.experimental import pallas as pl
from jax.experimental.pallas import tpu as pltpu

F32 = jnp.float32
BF16 = jnp.bfloat16

LANES = 128
MLA_HEADS = 16
QK_NOPE_DIM = 128
QK_ROPE_DIM = 64
QK_HEAD_DIM = QK_NOPE_DIM + QK_ROPE_DIM
Q_HEAD_PAD = 2 * LANES
V_HEAD_DIM = 128
Q_LORA_RANK = 1024
KV_LORA_RANK = 512
ROPE_BASE = 10000.0
GMLP_HEADS = 16
GMLP_CHUNK = 128
GMLP_HEAD_DIM = 128
PEER_HEADS = 8
PEER_NKEYS = 128
PEER_HALF = 128
PEER_TOPK = 16
PEER_SLOTS = PEER_HEADS * PEER_TOPK
NORM_EPS = 1e-6
VMEM_LIMIT = 56 * 1024 * 1024

_NT = (((1,), (1,)), ((), ()))


def _dot(a, b):
    return jnp.dot(a, b, preferred_element_type=F32)


def _dot_nt(a, b):
    return lax.dot_general(a, b, _NT, preferred_element_type=F32)


def _params(*sem):
    return pltpu.CompilerParams(dimension_semantics=sem, vmem_limit_bytes=VMEM_LIMIT)


def _rms(x, g):
    return x * lax.rsqrt(jnp.mean(x * x, axis=-1, keepdims=True) + NORM_EPS) * g


def _gelu(x):
    return 0.5 * x * (1.0 + lax.erf(x * math.sqrt(0.5)))


def _ada_kernel(c_ref, w_ref, b_ref, o_ref):
    c = c_ref[...]
    c_act = (c * jax.nn.sigmoid(c)).astype(BF16)
    o_ref[...] = _dot(c_act, w_ref[...].astype(BF16)) + b_ref[...]


def _ada(c_pad, w, b):
    rows, d = c_pad.shape
    n = w.shape[1]
    tn = 512
    return pl.pallas_call(
        _ada_kernel,
        grid=(n // tn,),
        in_specs=[
            pl.BlockSpec((rows, d), lambda j: (0, 0)),
            pl.BlockSpec((d, tn), lambda j: (0, j)),
            pl.BlockSpec((1, tn), lambda j: (0, j)),
        ],
        out_specs=pl.BlockSpec((rows, tn), lambda j: (0, j)),
        out_shape=jax.ShapeDtypeStruct((rows, n), F32),
        compiler_params=_params("parallel"),
        name="ada",
    )(c_pad, w, b.reshape(1, n))


def _rms_mod_kernel(x_ref, g_ref, sh_ref, sc_ref, o_ref):
    y = _rms(x_ref[...], g_ref[...])
    o_ref[...] = (y * (1.0 + sc_ref[...]) + sh_ref[...]).astype(o_ref.dtype)


def _rms_mod(x, g, shift, scale, seq, out_dtype):
    t, d = x.shape
    tm = 256
    per_batch = seq // tm
    mod_spec = pl.BlockSpec((None, 1, d), lambda i: (i // per_batch, 0, 0))
    return pl.pallas_call(
        _rms_mod_kernel,
        grid=(t // tm,),
        in_specs=[
            pl.BlockSpec((tm, d), lambda i: (i, 0)),
            pl.BlockSpec((1, d), lambda i: (0, 0)),
            mod_spec,
            mod_spec,
        ],
        out_specs=pl.BlockSpec((tm, d), lambda i: (i, 0)),
        out_shape=jax.ShapeDtypeStruct((t, d), out_dtype),
        compiler_params=_params("parallel"),
        name="rms_mod",
    )(x, g.reshape(1, d), shift[:, None, :], scale[:, None, :])


def _rms_kernel(x_ref, g_ref, o_ref):
    o_ref[...] = _rms(x_ref[...], g_ref[...]).astype(o_ref.dtype)


def _rms_rows(x, g, out_dtype):
    t, d = x.shape
    tm = 512
    return pl.pallas_call(
        _rms_kernel,
        grid=(t // tm,),
        in_specs=[pl.BlockSpec((tm, d), lambda i: (i, 0)), pl.BlockSpec((1, d), lambda i: (0, 0))],
        out_specs=pl.BlockSpec((tm, d), lambda i: (i, 0)),
        out_shape=jax.ShapeDtypeStruct((t, d), out_dtype),
        compiler_params=_params("parallel"),
        name="rms_rows",
    )(x, g.reshape(1, d))


def _final_kernel(x_ref, f_ref, gt_ref, g_ref, sh_ref, sc_ref, o_ref):
    x = x_ref[...] + gt_ref[...] * f_ref[...]
    o_ref[...] = _rms(x, g_ref[...]) * (1.0 + sc_ref[...]) + sh_ref[...]


def _final(x, f, gate, g, shift, scale, seq):
    t, d = x.shape
    tm = 256
    per_batch = seq // tm
    row = pl.BlockSpec((tm, d), lambda i: (i, 0))
    mod_spec = pl.BlockSpec((None, 1, d), lambda i: (i // per_batch, 0, 0))
    return pl.pallas_call(
        _final_kernel,
        grid=(t // tm,),
        in_specs=[row, row, mod_spec, pl.BlockSpec((1, d), lambda i: (0, 0)), mod_spec, mod_spec],
        out_specs=row,
        out_shape=jax.ShapeDtypeStruct((t, d), F32),
        compiler_params=_params("parallel"),
        name="final_norm",
    )(x, f, gate[:, None, :], g.reshape(1, d), shift[:, None, :], scale[:, None, :])


def _mm_kernel(a_ref, w_ref, o_ref):
    o_ref[...] = _dot(a_ref[...], w_ref[...].astype(BF16)).astype(o_ref.dtype)


def _matmul(a, w, tm, tn, out_dtype, name):
    m, k = a.shape
    n = w.shape[1]
    return pl.pallas_call(
        _mm_kernel,
        grid=(m // tm, n // tn),
        in_specs=[pl.BlockSpec((tm, k), lambda i, j: (i, 0)), pl.BlockSpec((k, tn), lambda i, j: (0, j))],
        out_specs=pl.BlockSpec((tm, tn), lambda i, j: (i, j)),
        out_shape=jax.ShapeDtypeStruct((m, n), out_dtype),
        compiler_params=_params("parallel", "parallel"),
        name=name,
    )(a, w)


def _swap_rope_halves(x):
    lane = lax.broadcasted_iota(jnp.int32, x.shape, 1)
    half = QK_ROPE_DIM // 2
    return jnp.where(lane < half, pltpu.roll(x, LANES - half, 1), pltpu.roll(x, half, 1))


def _rope_tile(x, cos_t, sin_t):
    return x * cos_t + _swap_rope_halves(x) * sin_t


def _q_proj_kernel(a_ref, w_ref, cos_ref, sin_ref, o_ref, *, heads_per_block, scale):
    acc = _dot(a_ref[...], w_ref[...])
    cos_t = cos_ref[...]
    sin_t = sin_ref[...]
    for h in range(heads_per_block):
        base = h * Q_HEAD_PAD
        nope = acc[:, base:base + LANES]
        pe = _rope_tile(acc[:, base + LANES:base + Q_HEAD_PAD], cos_t, sin_t)
        o_ref[:, base:base + LANES] = (nope * scale).astype(o_ref.dtype)
        o_ref[:, base + LANES:base + Q_HEAD_PAD] = (pe * scale).astype(o_ref.dtype)


def _q_proj(qn, w_uq_pad, cos_t, sin_t):
    m, k = qn.shape
    n = w_uq_pad.shape[1]
    tm, tn = 1024, 1024
    kern = functools.partial(_q_proj_kernel, heads_per_block=tn // Q_HEAD_PAD, scale=1.0 / math.sqrt(QK_HEAD_DIM))
    return pl.pallas_call(
        kern,
        grid=(m // tm, n // tn),
        in_specs=[
            pl.BlockSpec((tm, k), lambda i, j: (i, 0)),
            pl.BlockSpec((k, tn), lambda i, j: (0, j)),
            pl.BlockSpec((tm, LANES), lambda i, j: (i, 0)),
            pl.BlockSpec((tm, LANES), lambda i, j: (i, 0)),
        ],
        out_specs=pl.BlockSpec((tm, tn), lambda i, j: (i, j)),
        out_shape=jax.ShapeDtypeStruct((m, n), BF16),
        compiler_params=_params("parallel", "parallel"),
        name="q_proj",
    )(qn, w_uq_pad, cos_t, sin_t)


def _out_proj_kernel(a1_ref, a2_ref, w_ref, x_ref, gt_ref, o_ref):
    k1 = a1_ref.shape[1]
    w = w_ref[...].astype(BF16)
    acc = _dot(a1_ref[...], w[:k1]) + _dot(a2_ref[...], w[k1:])
    o_ref[...] = x_ref[...] + gt_ref[...] * acc


def _out_proj(a1, a2, w, x, gate, seq):
    m, k1 = a1.shape
    k2 = a2.shape[1]
    n = w.shape[1]
    tm, tn = 1024, 512
    per_batch = seq // tm
    return pl.pallas_call(
        _out_proj_kernel,
        grid=(m // tm, n // tn),
        in_specs=[
            pl.BlockSpec((tm, k1), lambda i, j: (i, 0)),
            pl.BlockSpec((tm, k2), lambda i, j: (i, 0)),
            pl.BlockSpec((k1 + k2, tn), lambda i, j: (0, j)),
            pl.BlockSpec((tm, tn), lambda i, j: (i, j)),
            pl.BlockSpec((None, 1, tn), lambda i, j: (i // per_batch, 0, j)),
        ],
        out_specs=pl.BlockSpec((tm, tn), lambda i, j: (i, j)),
        out_shape=jax.ShapeDtypeStruct((m, n), F32),
        compiler_params=_params("parallel", "parallel"),
        name="out_proj",
    )(a1, a2, w, x, gate[:, None, :])


def _latent_prep_kernel(z_ref, pos_ref, gq_ref, gkv_ref, freq_ref, cmask_ref, smask_ref,
                        qn_ref, kvn_ref, kpe_ref, cos_ref, sin_ref):
    q_lat = z_ref[:, :Q_LORA_RANK]
    kv_lat = z_ref[:, Q_LORA_RANK:Q_LORA_RANK + KV_LORA_RANK]
    k_pe = z_ref[:, Q_LORA_RANK + KV_LORA_RANK:]
    qn_ref[...] = _rms(q_lat, gq_ref[...]).astype(qn_ref.dtype)
    kvn_ref[...] = _rms(kv_lat, gkv_ref[...]).astype(kvn_ref.dtype)
    ang = pos_ref[...].astype(F32) * freq_ref[...]
    cos_t = jnp.cos(ang) * cmask_ref[...]
    sin_t = jnp.sin(ang) * smask_ref[...]
    cos_ref[...] = cos_t
    sin_ref[...] = sin_t
    kpe_ref[...] = _rope_tile(k_pe, cos_t, sin_t).astype(kpe_ref.dtype)


def _latent_prep(z_a, pos, g_q, g_kv):
    t, width = z_a.shape
    tm = 512
    half = QK_ROPE_DIM // 2
    inv_freq = ROPE_BASE ** (-jnp.arange(0, QK_ROPE_DIM, 2, dtype=F32) / QK_ROPE_DIM)
    zeros = jnp.zeros((LANES - QK_ROPE_DIM,), F32)
    freq = jnp.concatenate([inv_freq, inv_freq, zeros]).reshape(1, LANES)
    cmask = jnp.concatenate([jnp.ones((QK_ROPE_DIM,), F32), zeros]).reshape(1, LANES)
    smask = jnp.concatenate([-jnp.ones((half,), F32), jnp.ones((half,), F32), zeros]).reshape(1, LANES)
    const = lambda n: pl.BlockSpec((1, n), lambda i: (0, 0))
    rows = lambda n: pl.BlockSpec((tm, n), lambda i: (i, 0))
    return pl.pallas_call(
        _latent_prep_kernel,
        grid=(t // tm,),
        in_specs=[rows(width), rows(1), const(Q_LORA_RANK), const(KV_LORA_RANK),
                  const(LANES), const(LANES), const(LANES)],
        out_specs=[rows(Q_LORA_RANK), rows(KV_LORA_RANK), rows(LANES), rows(LANES), rows(LANES)],
        out_shape=[
            jax.ShapeDtypeStruct((t, Q_LORA_RANK), BF16),
            jax.ShapeDtypeStruct((t, KV_LORA_RANK), BF16),
            jax.ShapeDtypeStruct((t, LANES), BF16),
            jax.ShapeDtypeStruct((t, LANES), F32),
            jax.ShapeDtypeStruct((t, LANES), F32),
        ],
        compiler_params=_params("parallel"),
        name="latent_prep",
    )(z_a, pos, g_q.reshape(1, -1), g_kv.reshape(1, -1), freq, cmask, smask)


ATTN_HEADS_PER_STEP = 2


def _attn_kernel(q_ref, kv_ref, kpe_ref, o_ref, k_ref, *, tq):
    i = pl.program_id(2)
    kv_w = QK_NOPE_DIM + V_HEAD_DIM

    @pl.when(i == 0)
    def _():
        for hh in range(ATTN_HEADS_PER_STEP):
            k_ref[hh, :, :QK_NOPE_DIM] = kv_ref[:, hh * kv_w:hh * kv_w + QK_NOPE_DIM]
            k_ref[hh, :, QK_NOPE_DIM:] = kpe_ref[...]

    qs = [q_ref[:, hh * Q_HEAD_PAD:(hh + 1) * Q_HEAD_PAD] for hh in range(ATTN_HEADS_PER_STEP)]

    def scores(j, hh):
        off = pl.multiple_of(j * tq, tq)
        return _dot_nt(qs[hh], k_ref[hh, pl.ds(off, tq), :])

    def update(j, s, carry, hh):
        off = pl.multiple_of(j * tq, tq)
        v = kv_ref[pl.ds(off, tq), hh * kv_w + QK_NOPE_DIM:(hh + 1) * kv_w]
        m, l, acc = carry
        m_new = jnp.maximum(m, jnp.max(s, axis=-1, keepdims=True))
        alpha = jnp.exp(m - m_new)
        p = jnp.exp(s - m_new)
        l = alpha * l + jnp.sum(p, axis=-1, keepdims=True)
        acc = alpha * acc + _dot(p.astype(BF16), v)
        return m_new, l, acc

    def pair_body(jj, carries):
        out = []
        for hh in range(ATTN_HEADS_PER_STEP):
            s0 = scores(2 * jj, hh)
            s1 = scores(2 * jj + 1, hh)
            out.append(update(2 * jj + 1, s1, update(2 * jj, s0, carries[hh], hh), hh))
        return tuple(out)

    def single_body(j, carries):
        return tuple(update(j, scores(j, hh), carries[hh], hh) for hh in range(ATTN_HEADS_PER_STEP))

    init = (jnp.full((tq, 1), -jnp.inf, F32), jnp.zeros((tq, 1), F32), jnp.zeros((tq, V_HEAD_DIM), F32))
    pairs = lax.shift_right_logical(i, 1)
    carries = lax.fori_loop(0, pairs, pair_body, (init,) * ATTN_HEADS_PER_STEP)
    carries = lax.fori_loop(2 * pairs, i, single_body, carries)
    for hh in range(ATTN_HEADS_PER_STEP):
        s = scores(i, hh)
        row = lax.broadcasted_iota(jnp.int32, s.shape, 0)
        col = lax.broadcasted_iota(jnp.int32, s.shape, 1)
        _, l, acc = update(i, jnp.where(row >= col, s, -jnp.inf), carries[hh], hh)
        o_ref[:, hh * V_HEAD_DIM:(hh + 1) * V_HEAD_DIM] = acc / l


def _attention(q, kv, kpe, batch, seq):
    tq = 512
    nq = seq // tq
    hps = ATTN_HEADS_PER_STEP
    kern = functools.partial(_attn_kernel, tq=tq)
    return pl.pallas_call(
        kern,
        grid=(batch, MLA_HEADS // hps, nq),
        in_specs=[
            pl.BlockSpec((tq, hps * Q_HEAD_PAD), lambda b, h, i: (b * nq + i, h)),
            pl.BlockSpec((seq, hps * (QK_NOPE_DIM + V_HEAD_DIM)), lambda b, h, i: (b, h)),
            pl.BlockSpec((seq, LANES), lambda b, h, i: (b, 0)),
        ],
        out_specs=pl.BlockSpec((tq, hps * V_HEAD_DIM), lambda b, h, i: (b * nq + i, h)),
        out_shape=jax.ShapeDtypeStruct((batch * seq, MLA_HEADS * V_HEAD_DIM), F32),
        scratch_shapes=[pltpu.VMEM((hps, seq, Q_HEAD_PAD), BF16)],
        compiler_params=_params("parallel", "parallel", "arbitrary"),
        name="mla_attention",
    )(q, kv, kpe)


def _gmlp_kernel(z_ref, gs_ref, w_ref, bt_ref, beta_ref, o_ref, y_ref):
    rows = z_ref.shape[0]
    width = GMLP_HEADS * GMLP_HEAD_DIM
    u = _gelu(z_ref[:, :width])
    v = _gelu(z_ref[:, width:])
    mu = jnp.mean(v, axis=-1, keepdims=True)
    var = jnp.mean(jnp.square(v - mu), axis=-1, keepdims=True)
    vn = ((v - mu) * lax.rsqrt(var + NORM_EPS) * gs_ref[...]).astype(BF16)
    t_idx = lax.broadcasted_iota(jnp.int32, (GMLP_CHUNK, GMLP_CHUNK), 0)
    s_idx = lax.broadcasted_iota(jnp.int32, (GMLP_CHUNK, GMLP_CHUNK), 1)
    causal = (t_idx >= s_idx).astype(F32)
    for h in range(GMLP_HEADS):
        w = (w_ref[h] * causal).astype(BF16)
        bias = bt_ref[:, h:h + 1]
        cols = slice(h * GMLP_HEAD_DIM, (h + 1) * GMLP_HEAD_DIM)
        for c in range(rows // GMLP_CHUNK):
            rs = slice(c * GMLP_CHUNK, (c + 1) * GMLP_CHUNK)
            sv = _dot(w, vn[rs, cols]) + bias
            y_ref[rs, cols] = u[rs, cols] * sv
    o_ref[...] = _rms(y_ref[...], beta_ref[...]).astype(o_ref.dtype)


def _gmlp(z_g, g_sgu, w_sgu, b_sgu, beta):
    t, two_w = z_g.shape
    width = two_w // 2
    tm = 256
    return pl.pallas_call(
        _gmlp_kernel,
        grid=(t // tm,),
        in_specs=[
            pl.BlockSpec((tm, two_w), lambda i: (i, 0)),
            pl.BlockSpec((1, width), lambda i: (0, 0)),
            pl.BlockSpec((GMLP_HEADS, GMLP_CHUNK, GMLP_CHUNK), lambda i: (0, 0, 0)),
            pl.BlockSpec((GMLP_CHUNK, GMLP_HEADS), lambda i: (0, 0)),
            pl.BlockSpec((1, width), lambda i: (0, 0)),
        ],
        out_specs=pl.BlockSpec((tm, width), lambda i: (i, 0)),
        out_shape=jax.ShapeDtypeStruct((t, width), BF16),
        scratch_shapes=[pltpu.VMEM((tm, width), F32)],
        compiler_params=_params("parallel"),
        name="gmlp_sgu",
    )(z_g, g_sgu.reshape(1, width), w_sgu, jnp.transpose(b_sgu), beta.reshape(1, width))


ROUTE_TOKENS = 128
SCATTER_GROUP = 16
STAGE_PITCH = 136


def _top_k_rows(x, k, payload=None):
    n = x.shape[0]
    row = lax.broadcasted_iota(jnp.int32, x.shape, 0).astype(F32)
    vals, picks = [], []
    for _ in range(k):
        m = jnp.max(x, axis=0, keepdims=True)
        first = jnp.min(jnp.where(x == m, row, float(n)), axis=0, keepdims=True)
        sel = row == first
        vals.append(m)
        picks.append(first if payload is None else jnp.max(jnp.where(sel, payload, -1.0), axis=0, keepdims=True))
        x = jnp.where(sel, -jnp.inf, x)
    return jnp.concatenate(vals, axis=0), jnp.concatenate(picks, axis=0)


def _pair_candidates(v0, v1, i0, i1):
    k = PEER_TOPK
    tile = 8
    b_idx = lax.broadcasted_iota(jnp.int32, (tile, v0.shape[1]), 0)
    sums = [v0[0:1] + v1]
    ids = [i0[0:1] * float(PEER_NKEYS) + i1]
    for a in range(1, tile):
        sums.append(jnp.where(b_idx < k // (a + 1), v0[a:a + 1] + v1[0:tile], -jnp.inf))
        ids.append(i0[a:a + 1] * float(PEER_NKEYS) + i1[0:tile])
    sums.append(v0[tile:k] + v1[0:1])
    ids.append(i0[tile:k] * float(PEER_NKEYS) + i1[0:1])
    return jnp.concatenate(sums, axis=0), jnp.concatenate(ids, axis=0)


def _route_kernel(q_ref, keys_ref, w_ref, gate_ref, row_ref, col_ref, *stage_refs):
    gates, experts = [], []
    for h in range(PEER_HEADS):
        sub_v, sub_i = [], []
        for p in range(2):
            q_hp = q_ref[:, (2 * h + p) * PEER_HALF:(2 * h + p + 1) * PEER_HALF]
            s = _dot_nt(keys_ref[h, p], q_hp)
            v, idx = _top_k_rows(s, PEER_TOPK)
            sub_v.append(v)
            sub_i.append(idx)
        cand, cidx = _pair_candidates(sub_v[0], sub_v[1], sub_i[0], sub_i[1])
        top_s, eidx = _top_k_rows(cand, PEER_TOPK, payload=cidx)
        e = jnp.exp(top_s - top_s[0:1])
        gates.append(e / jnp.sum(e, axis=0, keepdims=True))
        experts.append(eidx)
    gate_t = jnp.concatenate(gates, axis=0).T
    expert_t = jnp.concatenate(experts, axis=0).T.astype(jnp.int32)
    gate_ref[...] = gate_t
    row_ref[...] = lax.shift_right_logical(expert_t, PEER_NKEYS.bit_length() - 1)
    col_ref[...] = lax.bitwise_and(expert_t, PEER_NKEYS - 1)

    sub = lax.broadcasted_iota(jnp.int32, (PEER_NKEYS, PEER_SLOTS), 0)

    half = SCATTER_GROUP // 2
    for gi in range(ROUTE_TOKENS // SCATTER_GROUP):
        stage_ref = stage_refs[gi % 2]
        t0 = gi * SCATTER_GROUP
        for tl in range(SCATTER_GROUP):
            g = jnp.broadcast_to(gate_ref[t0 + tl:t0 + tl + 1, :], sub.shape)
            r = jnp.broadcast_to(row_ref[t0 + tl:t0 + tl + 1, :], sub.shape)
            c = jnp.broadcast_to(col_ref[t0 + tl:t0 + tl + 1, :], sub.shape)
            a = jnp.where(r == sub, g, 0.0).astype(BF16)
            b = jnp.where(c == sub, 1.0, 0.0).astype(BF16)
            stage_ref[tl * STAGE_PITCH:tl * STAGE_PITCH + PEER_NKEYS, :] = _dot_nt(a, b)
        for i in range(PEER_NKEYS):
            lo = stage_ref[pl.ds(i, half, stride=STAGE_PITCH), :]
            hi = stage_ref[pl.ds(half * STAGE_PITCH + i, half, stride=STAGE_PITCH), :]
            w_ref[t0:t0 + SCATTER_GROUP, i * PEER_NKEYS:(i + 1) * PEER_NKEYS] = (
                jnp.concatenate([lo, hi], axis=0).astype(w_ref.dtype))


def _route(q, keys):
    t, width = q.shape
    n_exp = PEER_NKEYS * PEER_NKEYS
    tr = ROUTE_TOKENS
    return pl.pallas_call(
        _route_kernel,
        grid=(t // tr,),
        in_specs=[
            pl.BlockSpec((tr, width), lambda i: (i, 0)),
            pl.BlockSpec(keys.shape, lambda i: (0, 0, 0, 0)),
        ],
        out_specs=pl.BlockSpec((tr, n_exp), lambda i: (i, 0)),
        out_shape=jax.ShapeDtypeStruct((t, n_exp), BF16),
        scratch_shapes=[
            pltpu.VMEM((tr, PEER_SLOTS), F32),
            pltpu.VMEM((tr, PEER_SLOTS), jnp.int32),
            pltpu.VMEM((tr, PEER_SLOTS), jnp.int32),
            pltpu.VMEM((SCATTER_GROUP * STAGE_PITCH, PEER_NKEYS), F32),
            pltpu.VMEM((SCATTER_GROUP * STAGE_PITCH, PEER_NKEYS), F32),
        ],
        compiler_params=_params("parallel"),
        name="peer_route",
    )(q, keys)


EXPERT_COLS = 512


def _expert_kernel(h_ref, gate_ref, u_ref, v_ref, o_ref, p_ref):
    c = pl.program_id(1)

    @pl.when(c == 0)
    def _():
        o_ref[...] = jnp.zeros_like(o_ref)
        p_ref[1] = jnp.zeros(p_ref.shape[1:], p_ref.dtype)

    p_prev = p_ref[(c + 1) % 2]
    for n in range(o_ref.shape[1] // EXPERT_COLS):
        cols = slice(n * EXPERT_COLS, (n + 1) * EXPERT_COLS)
        o_ref[:, cols] += _dot(p_prev, v_ref[:, cols].astype(BF16))
    a = _gelu(_dot_nt(h_ref[...], u_ref[...].astype(BF16)))
    p_ref[c % 2] = (gate_ref[...].astype(F32) * a).astype(BF16)


def _experts(h, gate, expert_u, expert_v):
    t, d = h.shape
    n_exp = expert_u.shape[0]
    tb, ec = 1024, 256
    nc = n_exp // ec
    once = pl.Buffered(1)
    return pl.pallas_call(
        _expert_kernel,
        grid=(t // tb, nc + 1),
        in_specs=[
            pl.BlockSpec((tb, d), lambda i, c: (i, 0), pipeline_mode=once),
            pl.BlockSpec((tb, ec), lambda i, c: (i, jnp.minimum(c, nc - 1))),
            pl.BlockSpec((ec, d), lambda i, c: (jnp.minimum(c, nc - 1), 0)),
            pl.BlockSpec((ec, d), lambda i, c: (jnp.maximum(c - 1, 0), 0)),
        ],
        out_specs=pl.BlockSpec((tb, d), lambda i, c: (i, 0), pipeline_mode=once),
        out_shape=jax.ShapeDtypeStruct((t, d), F32),
        scratch_shapes=[pltpu.VMEM((2, tb, ec), BF16)],
        compiler_params=_params("parallel", "arbitrary"),
        name="peer_experts",
    )(h, gate, expert_u, expert_v)


def kernel(x, c, positions, w_ada, b_ada, g_norm_mix, w_in, g_q, w_uq, g_kv, w_ukv, g_sgu, w_sgu, b_sgu, beta_mla, beta_gmlp, w_out, g_norm_ffn, w_pq, peer_keys, expert_u, expert_v, w_ada_f, b_ada_f, g_norm_f):
    batch, seq, d = x.shape
    t = batch * seq
    assert w_ada.shape[0] == 1, "single-layer trunk"
    xf = x.reshape(t, d)

    c_pad = jnp.pad(c, ((0, 16 - batch), (0, 0)))
    mod = _ada(c_pad, w_ada[0], b_ada[0])[:batch]
    sh_a, sc_a, gt_a, sh_f, sc_f, gt_f = jnp.split(mod, 6, axis=-1)
    mod_f = _ada(c_pad, w_ada_f, b_ada_f)[:batch]
    sh_o, sc_o = jnp.split(mod_f, 2, axis=-1)

    lat = Q_LORA_RANK + KV_LORA_RANK + QK_ROPE_DIM
    w_in_a = jnp.pad(w_in[0][:, :lat], ((0, 0), (0, LANES - QK_ROPE_DIM))).astype(BF16)
    w_in_g = w_in[0][:, lat:].astype(BF16)
    w_uq_pad = jnp.pad(w_uq[0].reshape(Q_LORA_RANK, MLA_HEADS, QK_HEAD_DIM),
                       ((0, 0), (0, 0), (0, Q_HEAD_PAD - QK_HEAD_DIM))).reshape(Q_LORA_RANK, MLA_HEADS * Q_HEAD_PAD).astype(BF16)

    h1 = _rms_mod(xf, g_norm_mix[0], sh_a, sc_a, seq, BF16)
    z_a = _matmul(h1, w_in_a, 512, w_in_a.shape[1], F32, "in_proj_latent")
    z_g = _matmul(h1, w_in_g, 1024, 1024, F32, "in_proj_gmlp")
    qn, kvn, kpe, cos_t, sin_t = _latent_prep(z_a, positions.reshape(t, 1), g_q[0], g_kv[0])
    q = _q_proj(qn, w_uq_pad, cos_t, sin_t)
    kv = _matmul(kvn, w_ukv[0], 1024, 1024, BF16, "kv_proj")
    y_mla = _attention(q, kv, kpe, batch, seq)
    y1 = _rms_rows(y_mla, beta_mla[0], BF16)
    y2 = _gmlp(z_g, g_sgu[0], w_sgu[0], b_sgu[0], beta_gmlp[0])
    x2 = _out_proj(y1, y2, w_out[0], xf, gt_a, seq)

    h2 = _rms_mod(x2, g_norm_ffn[0], sh_f, sc_f, seq, BF16)
    pq = _matmul(h2, w_pq[0], 1024, 512, BF16, "peer_query")
    gate = _route(pq, peer_keys[0].astype(BF16))
    f = _experts(h2, gate, expert_u[0], expert_v[0])
    out = _final(x2, f, gt_f, g_norm_f, sh_o, sc_o, seq)
    return out.reshape(batch, seq, d)
```

```python
import functools
import math

import jax
import jax.numpy as jnp
from jax import lax
from jax.experimental import pallas as pl
from jax.experimental.pallas import tpu as pltpu

F32 = jnp.float32
BF16 = jnp.bfloat16

LANES = 128
MLA_HEADS = 16
QK_NOPE_DIM = 128
QK_ROPE_DIM = 64
QK_HEAD_DIM = QK_NOPE_DIM + QK_ROPE_DIM
Q_HEAD_PAD = 2 * LANES
V_HEAD_DIM = 128
Q_LORA_RANK = 1024
KV_LORA_RANK = 512
ROPE_BASE = 10000.0
GMLP_HEADS = 16
GMLP_CHUNK = 128
GMLP_HEAD_DIM = 128
PEER_HEADS = 8
PEER_NKEYS = 128
PEER_HALF = 128
PEER_TOPK = 16
PEER_SLOTS = PEER_HEADS * PEER_TOPK
NORM_EPS = 1e-6
VMEM_LIMIT = 56 * 1024 * 1024

_NT = (((1,), (1,)), ((), ()))


def _dot(a, b):
    return jnp.dot(a, b, preferred_element_type=F32)


def _dot_nt(a, b):
    return lax.dot_general(a, b, _NT, preferred_element_type=F32)


def _params(*sem):
    return pltpu.CompilerParams(dimension_semantics=sem, vmem_limit_bytes=VMEM_LIMIT)


def _rms(x, g):
    return x * lax.rsqrt(jnp.mean(x * x, axis=-1, keepdims=True) + NORM_EPS) * g


def _gelu(x):
    return 0.5 * x * (1.0 + lax.erf(x * math.sqrt(0.5)))


def _ada_kernel(c_ref, w_ref, b_ref, o_ref):
    c = c_ref[...]
    c_act = (c * jax.nn.sigmoid(c)).astype(BF16)
    o_ref[...] = _dot(c_act, w_ref[...].astype(BF16)) + b_ref[...]


def _ada(c_pad, w, b):
    rows, d = c_pad.shape
    n = w.shape[1]
    tn = 512
    return pl.pallas_call(
        _ada_kernel,
        grid=(n // tn,),
        in_specs=[
            pl.BlockSpec((rows, d), lambda j: (0, 0)),
            pl.BlockSpec((d, tn), lambda j: (0, j)),
            pl.BlockSpec((1, tn), lambda j: (0, j)),
        ],
        out_specs=pl.BlockSpec((rows, tn), lambda j: (0, j)),
        out_shape=jax.ShapeDtypeStruct((rows, n), F32),
        compiler_params=_params("parallel"),
        name="ada",
    )(c_pad, w, b.reshape(1, n))


def _rms_mod_kernel(x_ref, g_ref, sh_ref, sc_ref, o_ref):
    y = _rms(x_ref[...], g_ref[...])
    o_ref[...] = (y * (1.0 + sc_ref[...]) + sh_ref[...]).astype(o_ref.dtype)


def _rms_mod(x, g, shift, scale, seq, out_dtype):
    t, d = x.shape
    tm = 256
    per_batch = seq // tm
    mod_spec = pl.BlockSpec((None, 1, d), lambda i: (i // per_batch, 0, 0))
    return pl.pallas_call(
        _rms_mod_kernel,
        grid=(t // tm,),
        in_specs=[
            pl.BlockSpec((tm, d), lambda i: (i, 0)),
            pl.BlockSpec((1, d), lambda i: (0, 0)),
            mod_spec,
            mod_spec,
        ],
        out_specs=pl.BlockSpec((tm, d), lambda i: (i, 0)),
        out_shape=jax.ShapeDtypeStruct((t, d), out_dtype),
        compiler_params=_params("parallel"),
        name="rms_mod",
    )(x, g.reshape(1, d), shift[:, None, :], scale[:, None, :])


def _rms_kernel(x_ref, g_ref, o_ref):
    o_ref[...] = _rms(x_ref[...], g_ref[...]).astype(o_ref.dtype)


def _rms_rows(x, g, out_dtype):
    t, d = x.shape
    tm = 512
    return pl.pallas_call(
        _rms_kernel,
        grid=(t // tm,),
        in_specs=[pl.BlockSpec((tm, d), lambda i: (i, 0)), pl.BlockSpec((1, d), lambda i: (0, 0))],
        out_specs=pl.BlockSpec((tm, d), lambda i: (i, 0)),
        out_shape=jax.ShapeDtypeStruct((t, d), out_dtype),
        compiler_params=_params("parallel"),
        name="rms_rows",
    )(x, g.reshape(1, d))


def _final_kernel(x_ref, f_ref, gt_ref, g_ref, sh_ref, sc_ref, o_ref):
    x = x_ref[...] + gt_ref[...] * f_ref[...]
    o_ref[...] = _rms(x, g_ref[...]) * (1.0 + sc_ref[...]) + sh_ref[...]


def _final(x, f, gate, g, shift, scale, seq):
    t, d = x.shape
    tm = 256
    per_batch = seq // tm
    row = pl.BlockSpec((tm, d), lambda i: (i, 0))
    mod_spec = pl.BlockSpec((None, 1, d), lambda i: (i // per_batch, 0, 0))
    return pl.pallas_call(
        _final_kernel,
        grid=(t // tm,),
        in_specs=[row, row, mod_spec, pl.BlockSpec((1, d), lambda i: (0, 0)), mod_spec, mod_spec],
        out_specs=row,
        out_shape=jax.ShapeDtypeStruct((t, d), F32),
        compiler_params=_params("parallel"),
        name="final_norm",
    )(x, f, gate[:, None, :], g.reshape(1, d), shift[:, None, :], scale[:, None, :])


def _mm_kernel(a_ref, w_ref, o_ref):
    o_ref[...] = _dot(a_ref[...], w_ref[...].astype(BF16)).astype(o_ref.dtype)


def _matmul(a, w, tm, tn, out_dtype, name):
    m, k = a.shape
    n = w.shape[1]
    return pl.pallas_call(
        _mm_kernel,
        grid=(m // tm, n // tn),
        in_specs=[pl.BlockSpec((tm, k), lambda i, j: (i, 0)), pl.BlockSpec((k, tn), lambda i, j: (0, j))],
        out_specs=pl.BlockSpec((tm, tn), lambda i, j: (i, j)),
        out_shape=jax.ShapeDtypeStruct((m, n), out_dtype),
        compiler_params=_params("parallel", "parallel"),
        name=name,
    )(a, w)


def _swap_rope_halves(x):
    lane = lax.broadcasted_iota(jnp.int32, x.shape, 1)
    half = QK_ROPE_DIM // 2
    return jnp.where(lane < half, pltpu.roll(x, LANES - half, 1), pltpu.roll(x, half, 1))


def _rope_tile(x, cos_t, sin_t):
    return x * cos_t + _swap_rope_halves(x) * sin_t


def _q_proj_kernel(a_ref, w_ref, cos_ref, sin_ref, o_ref, *, heads_per_block, scale):
    acc = _dot(a_ref[...], w_ref[...])
    cos_t = cos_ref[...]
    sin_t = sin_ref[...]
    for h in range(heads_per_block):
        base = h * Q_HEAD_PAD
        nope = acc[:, base:base + LANES]
        pe = _rope_tile(acc[:, base + LANES:base + Q_HEAD_PAD], cos_t, sin_t)
        o_ref[:, base:base + LANES] = (nope * scale).astype(o_ref.dtype)
        o_ref[:, base + LANES:base + Q_HEAD_PAD] = (pe * scale).astype(o_ref.dtype)


def _q_proj(qn, w_uq_pad, cos_t, sin_t):
    m, k = qn.shape
    n = w_uq_pad.shape[1]
    tm, tn = 1024, 1024
    kern = functools.partial(_q_proj_kernel, heads_per_block=tn // Q_HEAD_PAD, scale=1.0 / math.sqrt(QK_HEAD_DIM))
    return pl.pallas_call(
        kern,
        grid=(m // tm, n // tn),
        in_specs=[
            pl.BlockSpec((tm, k), lambda i, j: (i, 0)),
            pl.BlockSpec((k, tn), lambda i, j: (0, j)),
            pl.BlockSpec((tm, LANES), lambda i, j: (i, 0)),
            pl.BlockSpec((tm, LANES), lambda i, j: (i, 0)),
        ],
        out_specs=pl.BlockSpec((tm, tn), lambda i, j: (i, j)),
        out_shape=jax.ShapeDtypeStruct((m, n), BF16),
        compiler_params=_params("parallel", "parallel"),
        name="q_proj",
    )(qn, w_uq_pad, cos_t, sin_t)


def _out_proj_kernel(a1_ref, a2_ref, w_ref, x_ref, gt_ref, o_ref):
    k1 = a1_ref.shape[1]
    w = w_ref[...].astype(BF16)
    acc = _dot(a1_ref[...], w[:k1]) + _dot(a2_ref[...], w[k1:])
    o_ref[...] = x_ref[...] + gt_ref[...] * acc


def _out_proj(a1, a2, w, x, gate, seq):
    m, k1 = a1.shape
    k2 = a2.shape[1]
    n = w.shape[1]
    tm, tn = 1024, 512
    per_batch = seq // tm
    return pl.pallas_call(
        _out_proj_kernel,
        grid=(m // tm, n // tn),
        in_specs=[
            pl.BlockSpec((tm, k1), lambda i, j: (i, 0)),
            pl.BlockSpec((tm, k2), lambda i, j: (i, 0)),
            pl.BlockSpec((k1 + k2, tn), lambda i, j: (0, j)),
            pl.BlockSpec((tm, tn), lambda i, j: (i, j)),
            pl.BlockSpec((None, 1, tn), lambda i, j: (i // per_batch, 0, j)),
        ],
        out_specs=pl.BlockSpec((tm, tn), lambda i, j: (i, j)),
        out_shape=jax.ShapeDtypeStruct((m, n), F32),
        compiler_params=_params("parallel", "parallel"),
        name="out_proj",
    )(a1, a2, w, x, gate[:, None, :])


def _latent_prep_kernel(z_ref, pos_ref, gq_ref, gkv_ref, freq_ref, cmask_ref, smask_ref,
                        qn_ref, kvn_ref, kpe_ref, cos_ref, sin_ref):
    q_lat = z_ref[:, :Q_LORA_RANK]
    kv_lat = z_ref[:, Q_LORA_RANK:Q_LORA_RANK + KV_LORA_RANK]
    k_pe = z_ref[:, Q_LORA_RANK + KV_LORA_RANK:]
    qn_ref[...] = _rms(q_lat, gq_ref[...]).astype(qn_ref.dtype)
    kvn_ref[...] = _rms(kv_lat, gkv_ref[...]).astype(kvn_ref.dtype)
    ang = pos_ref[...].astype(F32) * freq_ref[...]
    cos_t = jnp.cos(ang) * cmask_ref[...]
    sin_t = jnp.sin(ang) * smask_ref[...]
    cos_ref[...] = cos_t
    sin_ref[...] = sin_t
    kpe_ref[...] = _rope_tile(k_pe, cos_t, sin_t).astype(kpe_ref.dtype)


def _latent_prep(z_a, pos, g_q, g_kv):
    t, width = z_a.shape
    tm = 512
    half = QK_ROPE_DIM // 2
    inv_freq = ROPE_BASE ** (-jnp.arange(0, QK_ROPE_DIM, 2, dtype=F32) / QK_ROPE_DIM)
    zeros = jnp.zeros((LANES - QK_ROPE_DIM,), F32)
    freq = jnp.concatenate([inv_freq, inv_freq, zeros]).reshape(1, LANES)
    cmask = jnp.concatenate([jnp.ones((QK_ROPE_DIM,), F32), zeros]).reshape(1, LANES)
    smask = jnp.concatenate([-jnp.ones((half,), F32), jnp.ones((half,), F32), zeros]).reshape(1, LANES)
    const = lambda n: pl.BlockSpec((1, n), lambda i: (0, 0))
    rows = lambda n: pl.BlockSpec((tm, n), lambda i: (i, 0))
    return pl.pallas_call(
        _latent_prep_kernel,
        grid=(t // tm,),
        in_specs=[rows(width), rows(1), const(Q_LORA_RANK), const(KV_LORA_RANK),
                  const(LANES), const(LANES), const(LANES)],
        out_specs=[rows(Q_LORA_RANK), rows(KV_LORA_RANK), rows(LANES), rows(LANES), rows(LANES)],
        out_shape=[
            jax.ShapeDtypeStruct((t, Q_LORA_RANK), BF16),
            jax.ShapeDtypeStruct((t, KV_LORA_RANK), BF16),
            jax.ShapeDtypeStruct((t, LANES), BF16),
            jax.ShapeDtypeStruct((t, LANES), F32),
            jax.ShapeDtypeStruct((t, LANES), F32),
        ],
        compiler_params=_params("parallel"),
        name="latent_prep",
    )(z_a, pos, g_q.reshape(1, -1), g_kv.reshape(1, -1), freq, cmask, smask)


ATTN_HEADS_PER_STEP = 2


def _attn_kernel(q_ref, kv_ref, kpe_ref, o_ref, k_ref, *, tq):
    i = pl.program_id(2)
    kv_w = QK_NOPE_DIM + V_HEAD_DIM

    @pl.when(i == 0)
    def _():
        for hh in range(ATTN_HEADS_PER_STEP):
            k_ref[hh, :, :QK_NOPE_DIM] = kv_ref[:, hh * kv_w:hh * kv_w + QK_NOPE_DIM]
            k_ref[hh, :, QK_NOPE_DIM:] = kpe_ref[...]

    qs = [q_ref[:, hh * Q_HEAD_PAD:(hh + 1) * Q_HEAD_PAD] for hh in range(ATTN_HEADS_PER_STEP)]

    def scores(j, hh):
        off = pl.multiple_of(j * tq, tq)
        return _dot_nt(qs[hh], k_ref[hh, pl.ds(off, tq), :])

    def update(j, s, carry, hh):
        off = pl.multiple_of(j * tq, tq)
        v = kv_ref[pl.ds(off, tq), hh * kv_w + QK_NOPE_DIM:(hh + 1) * kv_w]
        m, l, acc = carry
        m_new = jnp.maximum(m, jnp.max(s, axis=-1, keepdims=True))
        alpha = jnp.exp(m - m_new)
        p = jnp.exp(s - m_new)
        l = alpha * l + jnp.sum(p, axis=-1, keepdims=True)
        acc = alpha * acc + _dot(p.astype(BF16), v)
        return m_new, l, acc

    def pair_body(jj, carries):
        out = []
        for hh in range(ATTN_HEADS_PER_STEP):
            s0 = scores(2 * jj, hh)
            s1 = scores(2 * jj + 1, hh)
            out.append(update(2 * jj + 1, s1, update(2 * jj, s0, carries[hh], hh), hh))
        return tuple(out)

    init = (jnp.full((tq, 1), -jnp.inf, F32), jnp.zeros((tq, 1), F32), jnp.zeros((tq, V_HEAD_DIM), F32))
    pairs = lax.shift_right_logical(i, 1)
    carries = lax.fori_loop(0, pairs, pair_body, (init,) * ATTN_HEADS_PER_STEP)

    def finish(with_unmasked_block):
        for hh in range(ATTN_HEADS_PER_STEP):
            carry = carries[hh]
            if with_unmasked_block:
                s_prev = scores(i - 1, hh)
            s = scores(i, hh)
            row = lax.broadcasted_iota(jnp.int32, s.shape, 0)
            col = lax.broadcasted_iota(jnp.int32, s.shape, 1)
            s = jnp.where(row >= col, s, -jnp.inf)
            if with_unmasked_block:
                carry = update(i - 1, s_prev, carry, hh)
            _, l, acc = update(i, s, carry, hh)
            o_ref[:, hh * V_HEAD_DIM:(hh + 1) * V_HEAD_DIM] = acc / l

    odd = lax.bitwise_and(i, 1)
    pl.when(odd == 1)(lambda: finish(True))
    pl.when(odd == 0)(lambda: finish(False))


def _attention(q, kv, kpe, batch, seq):
    tq = 512
    nq = seq // tq
    hps = ATTN_HEADS_PER_STEP
    kern = functools.partial(_attn_kernel, tq=tq)
    return pl.pallas_call(
        kern,
        grid=(batch, MLA_HEADS // hps, nq),
        in_specs=[
            pl.BlockSpec((tq, hps * Q_HEAD_PAD), lambda b, h, i: (b * nq + i, h)),
            pl.BlockSpec((seq, hps * (QK_NOPE_DIM + V_HEAD_DIM)), lambda b, h, i: (b, h)),
            pl.BlockSpec((seq, LANES), lambda b, h, i: (b, 0)),
        ],
        out_specs=pl.BlockSpec((tq, hps * V_HEAD_DIM), lambda b, h, i: (b * nq + i, h)),
        out_shape=jax.ShapeDtypeStruct((batch * seq, MLA_HEADS * V_HEAD_DIM), F32),
        scratch_shapes=[pltpu.VMEM((hps, seq, Q_HEAD_PAD), BF16)],
        compiler_params=_params("parallel", "parallel", "arbitrary"),
        name="mla_attention",
    )(q, kv, kpe)


def _gmlp_kernel(z_ref, gs_ref, w_ref, bt_ref, beta_ref, o_ref, y_ref):
    rows = z_ref.shape[0]
    width = GMLP_HEADS * GMLP_HEAD_DIM
    u = _gelu(z_ref[:, :width])
    v = _gelu(z_ref[:, width:])
    mu = jnp.mean(v, axis=-1, keepdims=True)
    var = jnp.mean(jnp.square(v - mu), axis=-1, keepdims=True)
    vn = ((v - mu) * lax.rsqrt(var + NORM_EPS) * gs_ref[...]).astype(BF16)
    t_idx = lax.broadcasted_iota(jnp.int32, (GMLP_CHUNK, GMLP_CHUNK), 0)
    s_idx = lax.broadcasted_iota(jnp.int32, (GMLP_CHUNK, GMLP_CHUNK), 1)
    causal = (t_idx >= s_idx).astype(F32)
    for h in range(GMLP_HEADS):
        w = (w_ref[h] * causal).astype(BF16)
        bias = bt_ref[:, h:h + 1]
        cols = slice(h * GMLP_HEAD_DIM, (h + 1) * GMLP_HEAD_DIM)
        for c in range(rows // GMLP_CHUNK):
            rs = slice(c * GMLP_CHUNK, (c + 1) * GMLP_CHUNK)
            sv = _dot(w, vn[rs, cols]) + bias
            y_ref[rs, cols] = u[rs, cols] * sv
    o_ref[...] = _rms(y_ref[...], beta_ref[...]).astype(o_ref.dtype)


def _gmlp(z_g, g_sgu, w_sgu, b_sgu, beta):
    t, two_w = z_g.shape
    width = two_w // 2
    tm = 256
    return pl.pallas_call(
        _gmlp_kernel,
        grid=(t // tm,),
        in_specs=[
            pl.BlockSpec((tm, two_w), lambda i: (i, 0)),
            pl.BlockSpec((1, width), lambda i: (0, 0)),
            pl.BlockSpec((GMLP_HEADS, GMLP_CHUNK, GMLP_CHUNK), lambda i: (0, 0, 0)),
            pl.BlockSpec((GMLP_CHUNK, GMLP_HEADS), lambda i: (0, 0)),
            pl.BlockSpec((1, width), lambda i: (0, 0)),
        ],
        out_specs=pl.BlockSpec((tm, width), lambda i: (i, 0)),
        out_shape=jax.ShapeDtypeStruct((t, width), BF16),
        scratch_shapes=[pltpu.VMEM((tm, width), F32)],
        compiler_params=_params("parallel"),
        name="gmlp_sgu",
    )(z_g, g_sgu.reshape(1, width), w_sgu, jnp.transpose(b_sgu), beta.reshape(1, width))


ROUTE_TOKENS = 128
SCATTER_GROUP = 16
STAGE_PITCH = 136


def _top_k_rows(x, k, payload=None):
    n = x.shape[0]
    row = lax.broadcasted_iota(jnp.int32, x.shape, 0).astype(F32)
    vals, picks = [], []
    for _ in range(k):
        m = jnp.max(x, axis=0, keepdims=True)
        first = jnp.min(jnp.where(x == m, row, float(n)), axis=0, keepdims=True)
        sel = row == first
        vals.append(m)
        picks.append(first if payload is None else jnp.max(jnp.where(sel, payload, -1.0), axis=0, keepdims=True))
        x = jnp.where(sel, -jnp.inf, x)
    return jnp.concatenate(vals, axis=0), jnp.concatenate(picks, axis=0)


def _pair_candidates(v0, v1, i0, i1):
    k = PEER_TOPK
    tile = 8
    b_idx = lax.broadcasted_iota(jnp.int32, (tile, v0.shape[1]), 0)
    sums = [v0[0:1] + v1]
    ids = [i0[0:1] * float(PEER_NKEYS) + i1]
    for a in range(1, tile):
        sums.append(jnp.where(b_idx < k // (a + 1), v0[a:a + 1] + v1[0:tile], -jnp.inf))
        ids.append(i0[a:a + 1] * float(PEER_NKEYS) + i1[0:tile])
    sums.append(v0[tile:k] + v1[0:1])
    ids.append(i0[tile:k] * float(PEER_NKEYS) + i1[0:1])
    return jnp.concatenate(sums, axis=0), jnp.concatenate(ids, axis=0)


def _route_kernel(q_ref, keys_ref, w_ref, gate_ref, row_ref, col_ref, *stage_refs):
    gates, experts = [], []
    for h in range(PEER_HEADS):
        sub_v, sub_i = [], []
        for p in range(2):
            q_hp = q_ref[:, (2 * h + p) * PEER_HALF:(2 * h + p + 1) * PEER_HALF]
            s = _dot_nt(keys_ref[h, p], q_hp)
            v, idx = _top_k_rows(s, PEER_TOPK)
            sub_v.append(v)
            sub_i.append(idx)
        cand, cidx = _pair_candidates(sub_v[0], sub_v[1], sub_i[0], sub_i[1])
        top_s, eidx = _top_k_rows(cand, PEER_TOPK, payload=cidx)
        e = jnp.exp(top_s - top_s[0:1])
        gates.append(e / jnp.sum(e, axis=0, keepdims=True))
        experts.append(eidx)
    gate_t = jnp.concatenate(gates, axis=0).T
    expert_t = jnp.concatenate(experts, axis=0).T.astype(jnp.int32)
    gate_ref[...] = gate_t
    row_ref[...] = lax.shift_right_logical(expert_t, PEER_NKEYS.bit_length() - 1)
    col_ref[...] = lax.bitwise_and(expert_t, PEER_NKEYS - 1)

    sub = lax.broadcasted_iota(jnp.int32, (PEER_NKEYS, PEER_SLOTS), 0)

    half = SCATTER_GROUP // 2
    for gi in range(ROUTE_TOKENS // SCATTER_GROUP):
        stage_ref = stage_refs[gi % 2]
        t0 = gi * SCATTER_GROUP
        for tl in range(SCATTER_GROUP):
            g = jnp.broadcast_to(gate_ref[t0 + tl:t0 + tl + 1, :], sub.shape)
            r = jnp.broadcast_to(row_ref[t0 + tl:t0 + tl + 1, :], sub.shape)
            c = jnp.broadcast_to(col_ref[t0 + tl:t0 + tl + 1, :], sub.shape)
            a = jnp.where(r == sub, g, 0.0).astype(BF16)
            b = jnp.where(c == sub, 1.0, 0.0).astype(BF16)
            stage_ref[tl * STAGE_PITCH:tl * STAGE_PITCH + PEER_NKEYS, :] = _dot_nt(a, b)
        for i in range(PEER_NKEYS):
            lo = stage_ref[pl.ds(i, half, stride=STAGE_PITCH), :]
            hi = stage_ref[pl.ds(half * STAGE_PITCH + i, half, stride=STAGE_PITCH), :]
            w_ref[t0:t0 + SCATTER_GROUP, i * PEER_NKEYS:(i + 1) * PEER_NKEYS] = (
                jnp.concatenate([lo, hi], axis=0).astype(w_ref.dtype))


def _route(q, keys):
    t, width = q.shape
    n_exp = PEER_NKEYS * PEER_NKEYS
    tr = ROUTE_TOKENS
    return pl.pallas_call(
        _route_kernel,
        grid=(t // tr,),
        in_specs=[
            pl.BlockSpec((tr, width), lambda i: (i, 0)),
            pl.BlockSpec(keys.shape, lambda i: (0, 0, 0, 0)),
        ],
        out_specs=pl.BlockSpec((tr, n_exp), lambda i: (i, 0)),
        out_shape=jax.ShapeDtypeStruct((t, n_exp), BF16),
        scratch_shapes=[
            pltpu.VMEM((tr, PEER_SLOTS), F32),
            pltpu.VMEM((tr, PEER_SLOTS), jnp.int32),
            pltpu.VMEM((tr, PEER_SLOTS), jnp.int32),
            pltpu.VMEM((SCATTER_GROUP * STAGE_PITCH, PEER_NKEYS), F32),
            pltpu.VMEM((SCATTER_GROUP * STAGE_PITCH, PEER_NKEYS), F32),
        ],
        compiler_params=_params("parallel"),
        name="peer_route",
    )(q, keys)


EXPERT_COLS = 512


def _expert_kernel(h_ref, gate_ref, u_ref, v_ref, o_ref, p_ref):
    c = pl.program_id(1)

    @pl.when(c == 0)
    def _():
        o_ref[...] = jnp.zeros_like(o_ref)
        p_ref[1] = jnp.zeros(p_ref.shape[1:], p_ref.dtype)

    p_prev = p_ref[(c + 1) % 2]
    for n in range(o_ref.shape[1] // EXPERT_COLS):
        cols = slice(n * EXPERT_COLS, (n + 1) * EXPERT_COLS)
        o_ref[:, cols] += _dot(p_prev, v_ref[:, cols].astype(BF16))
    a = _gelu(_dot_nt(h_ref[...], u_ref[...].astype(BF16)))
    p_ref[c % 2] = (gate_ref[...].astype(F32) * a).astype(BF16)


def _experts(h, gate, expert_u, expert_v):
    t, d = h.shape
    n_exp = expert_u.shape[0]
    tb, ec = 1024, 256
    nc = n_exp // ec
    once = pl.Buffered(1)
    return pl.pallas_call(
        _expert_kernel,
        grid=(t // tb, nc + 1),
        in_specs=[
            pl.BlockSpec((tb, d), lambda i, c: (i, 0), pipeline_mode=once),
            pl.BlockSpec((tb, ec), lambda i, c: (i, jnp.minimum(c, nc - 1))),
            pl.BlockSpec((ec, d), lambda i, c: (jnp.minimum(c, nc - 1), 0)),
            pl.BlockSpec((ec, d), lambda i, c: (jnp.maximum(c - 1, 0), 0)),
        ],
        out_specs=pl.BlockSpec((tb, d), lambda i, c: (i, 0), pipeline_mode=once),
        out_shape=jax.ShapeDtypeStruct((t, d), F32),
        scratch_shapes=[pltpu.VMEM((2, tb, ec), BF16)],
        compiler_params=_params("parallel", "arbitrary"),
        name="peer_experts",
    )(h, gate, expert_u, expert_v)


def kernel(x, c, positions, w_ada, b_ada, g_norm_mix, w_in, g_q, w_uq, g_kv, w_ukv, g_sgu, w_sgu, b_sgu, beta_mla, beta_gmlp, w_out, g_norm_ffn, w_pq, peer_keys, expert_u, expert_v, w_ada_f, b_ada_f, g_norm_f):
    batch, seq, d = x.shape
    t = batch * seq
    assert w_ada.shape[0] == 1, "single-layer trunk"
    xf = x.reshape(t, d)

    c_pad = jnp.pad(c, ((0, 16 - batch), (0, 0)))
    mod = _ada(c_pad, w_ada[0], b_ada[0])[:batch]
    sh_a, sc_a, gt_a, sh_f, sc_f, gt_f = jnp.split(mod, 6, axis=-1)
    mod_f = _ada(c_pad, w_ada_f, b_ada_f)[:batch]
    sh_o, sc_o = jnp.split(mod_f, 2, axis=-1)

    lat = Q_LORA_RANK + KV_LORA_RANK + QK_ROPE_DIM
    w_in_a = jnp.pad(w_in[0][:, :lat], ((0, 0), (0, LANES - QK_ROPE_DIM))).astype(BF16)
    w_in_g = w_in[0][:, lat:].astype(BF16)
    w_uq_pad = jnp.pad(w_uq[0].reshape(Q_LORA_RANK, MLA_HEADS, QK_HEAD_DIM),
                       ((0, 0), (0, 0), (0, Q_HEAD_PAD - QK_HEAD_DIM))).reshape(Q_LORA_RANK, MLA_HEADS * Q_HEAD_PAD).astype(BF16)

    h1 = _rms_mod(xf, g_norm_mix[0], sh_a, sc_a, seq, BF16)
    z_a = _matmul(h1, w_in_a, 512, w_in_a.shape[1], F32, "in_proj_latent")
    z_g = _matmul(h1, w_in_g, 1024, 1024, F32, "in_proj_gmlp")
    qn, kvn, kpe, cos_t, sin_t = _latent_prep(z_a, positions.reshape(t, 1), g_q[0], g_kv[0])
    q = _q_proj(qn, w_uq_pad, cos_t, sin_t)
    kv = _matmul(kvn, w_ukv[0], 1024, 1024, BF16, "kv_proj")
    y_mla = _attention(q, kv, kpe, batch, seq)
    y1 = _rms_rows(y_mla, beta_mla[0], BF16)
    y2 = _gmlp(z_g, g_sgu[0], w_sgu[0], b_sgu[0], beta_gmlp[0])
    x2 = _out_proj(y1, y2, w_out[0], xf, gt_a, seq)

    h2 = _rms_mod(x2, g_norm_ffn[0], sh_f, sc_f, seq, BF16)
    pq = _matmul(h2, w_pq[0], 1024, 512, BF16, "peer_query")
    gate = _route(pq, peer_keys[0].astype(BF16))
    f = _experts(h2, gate, expert_u[0], expert_v[0])
    out = _final(x2, f, gt_f, g_norm_f, sh_o, sc_o, seq)
    return out.reshape(batch, seq, d)
```

```python
import functools
import math

import jax
import jax.numpy as jnp
from jax import lax
from jax.experimental import pallas as pl
from jax.experimental.pallas import tpu as pltpu

F32 = jnp.float32
BF16 = jnp.bfloat16

LANES = 128
MLA_HEADS = 16
QK_NOPE_DIM = 128
QK_ROPE_DIM = 64
QK_HEAD_DIM = QK_NOPE_DIM + QK_ROPE_DIM
Q_HEAD_PAD = 2 * LANES
V_HEAD_DIM = 128
Q_LORA_RANK = 1024
KV_LORA_RANK = 512
ROPE_BASE = 10000.0
GMLP_HEADS = 16
GMLP_CHUNK = 128
GMLP_HEAD_DIM = 128
PEER_HEADS = 8
PEER_NKEYS = 128
PEER_HALF = 128
PEER_TOPK = 16
PEER_SLOTS = PEER_HEADS * PEER_TOPK
NORM_EPS = 1e-6
VMEM_LIMIT = 56 * 1024 * 1024

_NT = (((1,), (1,)), ((), ()))


def _dot(a, b):
    return jnp.dot(a, b, preferred_element_type=F32)


def _dot_nt(a, b):
    return lax.dot_general(a, b, _NT, preferred_element_type=F32)


def _params(*sem):
    return pltpu.CompilerParams(dimension_semantics=sem, vmem_limit_bytes=VMEM_LIMIT)


def _rms(x, g):
    return x * lax.rsqrt(jnp.mean(x * x, axis=-1, keepdims=True) + NORM_EPS) * g


def _gelu(x):
    return 0.5 * x * (1.0 + lax.erf(x * math.sqrt(0.5)))


def _ada_kernel(c_ref, w_ref, b_ref, o_ref):
    c = c_ref[...]
    c_act = (c * jax.nn.sigmoid(c)).astype(BF16)
    o_ref[...] = _dot(c_act, w_ref[...].astype(BF16)) + b_ref[...]


def _ada(c_pad, w, b):
    rows, d = c_pad.shape
    n = w.shape[1]
    tn = 512
    return pl.pallas_call(
        _ada_kernel,
        grid=(n // tn,),
        in_specs=[
            pl.BlockSpec((rows, d), lambda j: (0, 0)),
            pl.BlockSpec((d, tn), lambda j: (0, j)),
            pl.BlockSpec((1, tn), lambda j: (0, j)),
        ],
        out_specs=pl.BlockSpec((rows, tn), lambda j: (0, j)),
        out_shape=jax.ShapeDtypeStruct((rows, n), F32),
        compiler_params=_params("parallel"),
        name="ada",
    )(c_pad, w, b.reshape(1, n))


def _rms_mod_kernel(x_ref, g_ref, sh_ref, sc_ref, o_ref):
    y = _rms(x_ref[...], g_ref[...])
    o_ref[...] = (y * (1.0 + sc_ref[...]) + sh_ref[...]).astype(o_ref.dtype)


def _rms_mod(x, g, shift, scale, seq, out_dtype):
    t, d = x.shape
    tm = 256
    per_batch = seq // tm
    mod_spec = pl.BlockSpec((None, 1, d), lambda i: (i // per_batch, 0, 0))
    return pl.pallas_call(
        _rms_mod_kernel,
        grid=(t // tm,),
        in_specs=[
            pl.BlockSpec((tm, d), lambda i: (i, 0)),
            pl.BlockSpec((1, d), lambda i: (0, 0)),
            mod_spec,
            mod_spec,
        ],
        out_specs=pl.BlockSpec((tm, d), lambda i: (i, 0)),
        out_shape=jax.ShapeDtypeStruct((t, d), out_dtype),
        compiler_params=_params("parallel"),
        name="rms_mod",
    )(x, g.reshape(1, d), shift[:, None, :], scale[:, None, :])


def _rms_kernel(x_ref, g_ref, o_ref):
    o_ref[...] = _rms(x_ref[...], g_ref[...]).astype(o_ref.dtype)


def _rms_rows(x, g, out_dtype):
    t, d = x.shape
    tm = 512
    return pl.pallas_call(
        _rms_kernel,
        grid=(t // tm,),
        in_specs=[pl.BlockSpec((tm, d), lambda i: (i, 0)), pl.BlockSpec((1, d), lambda i: (0, 0))],
        out_specs=pl.BlockSpec((tm, d), lambda i: (i, 0)),
        out_shape=jax.ShapeDtypeStruct((t, d), out_dtype),
        compiler_params=_params("parallel"),
        name="rms_rows",
    )(x, g.reshape(1, d))


def _final_kernel(x_ref, f_ref, gt_ref, g_ref, sh_ref, sc_ref, o_ref):
    x = x_ref[...] + gt_ref[...] * f_ref[...]
    o_ref[...] = _rms(x, g_ref[...]) * (1.0 + sc_ref[...]) + sh_ref[...]


def _final(x, f, gate, g, shift, scale, seq):
    t, d = x.shape
    tm = 256
    per_batch = seq // tm
    row = pl.BlockSpec((tm, d), lambda i: (i, 0))
    mod_spec = pl.BlockSpec((None, 1, d), lambda i: (i // per_batch, 0, 0))
    return pl.pallas_call(
        _final_kernel,
        grid=(t // tm,),
        in_specs=[row, row, mod_spec, pl.BlockSpec((1, d), lambda i: (0, 0)), mod_spec, mod_spec],
        out_specs=row,
        out_shape=jax.ShapeDtypeStruct((t, d), F32),
        compiler_params=_params("parallel"),
        name="final_norm",
    )(x, f, gate[:, None, :], g.reshape(1, d), shift[:, None, :], scale[:, None, :])


def _mm_kernel(a_ref, w_ref, o_ref):
    o_ref[...] = _dot(a_ref[...], w_ref[...].astype(BF16)).astype(o_ref.dtype)


def _matmul(a, w, tm, tn, out_dtype, name):
    m, k = a.shape
    n = w.shape[1]
    return pl.pallas_call(
        _mm_kernel,
        grid=(m // tm, n // tn),
        in_specs=[pl.BlockSpec((tm, k), lambda i, j: (i, 0)), pl.BlockSpec((k, tn), lambda i, j: (0, j))],
        out_specs=pl.BlockSpec((tm, tn), lambda i, j: (i, j)),
        out_shape=jax.ShapeDtypeStruct((m, n), out_dtype),
        compiler_params=_params("parallel", "parallel"),
        name=name,
    )(a, w)


def _swap_rope_halves(x):
    lane = lax.broadcasted_iota(jnp.int32, x.shape, 1)
    half = QK_ROPE_DIM // 2
    return jnp.where(lane < half, pltpu.roll(x, LANES - half, 1), pltpu.roll(x, half, 1))


def _rope_tile(x, cos_t, sin_t):
    return x * cos_t + _swap_rope_halves(x) * sin_t


def _q_proj_kernel(a_ref, w_ref, cos_ref, sin_ref, o_ref, *, heads_per_block, scale):
    acc = _dot(a_ref[...], w_ref[...])
    cos_t = cos_ref[...]
    sin_t = sin_ref[...]
    for h in range(heads_per_block):
        base = h * Q_HEAD_PAD
        nope = acc[:, base:base + LANES]
        pe = _rope_tile(acc[:, base + LANES:base + Q_HEAD_PAD], cos_t, sin_t)
        o_ref[:, base:base + LANES] = (nope * scale).astype(o_ref.dtype)
        o_ref[:, base + LANES:base + Q_HEAD_PAD] = (pe * scale).astype(o_ref.dtype)


def _q_proj(qn, w_uq_pad, cos_t, sin_t):
    m, k = qn.shape
    n = w_uq_pad.shape[1]
    tm, tn = 1024, 1024
    kern = functools.partial(_q_proj_kernel, heads_per_block=tn // Q_HEAD_PAD, scale=1.0 / math.sqrt(QK_HEAD_DIM))
    return pl.pallas_call(
        kern,
        grid=(m // tm, n // tn),
        in_specs=[
            pl.BlockSpec((tm, k), lambda i, j: (i, 0)),
            pl.BlockSpec((k, tn), lambda i, j: (0, j)),
            pl.BlockSpec((tm, LANES), lambda i, j: (i, 0)),
            pl.BlockSpec((tm, LANES), lambda i, j: (i, 0)),
        ],
        out_specs=pl.BlockSpec((tm, tn), lambda i, j: (i, j)),
        out_shape=jax.ShapeDtypeStruct((m, n), BF16),
        compiler_params=_params("parallel", "parallel"),
        name="q_proj",
    )(qn, w_uq_pad, cos_t, sin_t)


def _out_proj_kernel(a1_ref, a2_ref, w_ref, x_ref, gt_ref, o_ref):
    k1 = a1_ref.shape[1]
    w = w_ref[...].astype(BF16)
    acc = _dot(a1_ref[...], w[:k1]) + _dot(a2_ref[...], w[k1:])
    o_ref[...] = x_ref[...] + gt_ref[...] * acc


def _out_proj(a1, a2, w, x, gate, seq):
    m, k1 = a1.shape
    k2 = a2.shape[1]
    n = w.shape[1]
    tm, tn = 1024, 512
    per_batch = seq // tm
    return pl.pallas_call(
        _out_proj_kernel,
        grid=(m // tm, n // tn),
        in_specs=[
            pl.BlockSpec((tm, k1), lambda i, j: (i, 0)),
            pl.BlockSpec((tm, k2), lambda i, j: (i, 0)),
            pl.BlockSpec((k1 + k2, tn), lambda i, j: (0, j)),
            pl.BlockSpec((tm, tn), lambda i, j: (i, j)),
            pl.BlockSpec((None, 1, tn), lambda i, j: (i // per_batch, 0, j)),
        ],
        out_specs=pl.BlockSpec((tm, tn), lambda i, j: (i, j)),
        out_shape=jax.ShapeDtypeStruct((m, n), F32),
        compiler_params=_params("parallel", "parallel"),
        name="out_proj",
    )(a1, a2, w, x, gate[:, None, :])


def _latent_prep_kernel(z_ref, pos_ref, gq_ref, gkv_ref, freq_ref, cmask_ref, smask_ref,
                        qn_ref, kvn_ref, kpe_ref, cos_ref, sin_ref):
    q_lat = z_ref[:, :Q_LORA_RANK]
    kv_lat = z_ref[:, Q_LORA_RANK:Q_LORA_RANK + KV_LORA_RANK]
    k_pe = z_ref[:, Q_LORA_RANK + KV_LORA_RANK:]
    qn_ref[...] = _rms(q_lat, gq_ref[...]).astype(qn_ref.dtype)
    kvn_ref[...] = _rms(kv_lat, gkv_ref[...]).astype(kvn_ref.dtype)
    ang = pos_ref[...].astype(F32) * freq_ref[...]
    cos_t = jnp.cos(ang) * cmask_ref[...]
    sin_t = jnp.sin(ang) * smask_ref[...]
    cos_ref[...] = cos_t
    sin_ref[...] = sin_t
    kpe_ref[...] = _rope_tile(k_pe, cos_t, sin_t).astype(kpe_ref.dtype)


def _latent_prep(z_a, pos, g_q, g_kv):
    t, width = z_a.shape
    tm = 512
    half = QK_ROPE_DIM // 2
    inv_freq = ROPE_BASE ** (-jnp.arange(0, QK_ROPE_DIM, 2, dtype=F32) / QK_ROPE_DIM)
    zeros = jnp.zeros((LANES - QK_ROPE_DIM,), F32)
    freq = jnp.concatenate([inv_freq, inv_freq, zeros]).reshape(1, LANES)
    cmask = jnp.concatenate([jnp.ones((QK_ROPE_DIM,), F32), zeros]).reshape(1, LANES)
    smask = jnp.concatenate([-jnp.ones((half,), F32), jnp.ones((half,), F32), zeros]).reshape(1, LANES)
    const = lambda n: pl.BlockSpec((1, n), lambda i: (0, 0))
    rows = lambda n: pl.BlockSpec((tm, n), lambda i: (i, 0))
    return pl.pallas_call(
        _latent_prep_kernel,
        grid=(t // tm,),
        in_specs=[rows(width), rows(1), const(Q_LORA_RANK), const(KV_LORA_RANK),
                  const(LANES), const(LANES), const(LANES)],
        out_specs=[rows(Q_LORA_RANK), rows(KV_LORA_RANK), rows(LANES), rows(LANES), rows(LANES)],
        out_shape=[
            jax.ShapeDtypeStruct((t, Q_LORA_RANK), BF16),
            jax.ShapeDtypeStruct((t, KV_LORA_RANK), BF16),
            jax.ShapeDtypeStruct((t, LANES), BF16),
            jax.ShapeDtypeStruct((t, LANES), F32),
            jax.ShapeDtypeStruct((t, LANES), F32),
        ],
        compiler_params=_params("parallel"),
        name="latent_prep",
    )(z_a, pos, g_q.reshape(1, -1), g_kv.reshape(1, -1), freq, cmask, smask)


ATTN_HEADS_PER_STEP = 2


def _attn_kernel(q_ref, kv_ref, kpe_ref, o_ref, k_ref, *, tq):
    i = pl.program_id(2)
    kv_w = QK_NOPE_DIM + V_HEAD_DIM

    @pl.when(i == 0)
    def _():
        for hh in range(ATTN_HEADS_PER_STEP):
            k_ref[hh, :, :QK_NOPE_DIM] = kv_ref[:, hh * kv_w:hh * kv_w + QK_NOPE_DIM]
            k_ref[hh, :, QK_NOPE_DIM:] = kpe_ref[...]

    qs = [q_ref[:, hh * Q_HEAD_PAD:(hh + 1) * Q_HEAD_PAD] for hh in range(ATTN_HEADS_PER_STEP)]

    def scores(j, hh):
        off = pl.multiple_of(j * tq, tq)
        return _dot_nt(qs[hh], k_ref[hh, pl.ds(off, tq), :])

    def update(j, s, carry, hh):
        off = pl.multiple_of(j * tq, tq)
        v = kv_ref[pl.ds(off, tq), hh * kv_w + QK_NOPE_DIM:(hh + 1) * kv_w]
        m, l, acc = carry
        m_new = jnp.maximum(m, jnp.max(s, axis=-1, keepdims=True))
        alpha = jnp.exp(m - m_new)
        p = jnp.exp(s - m_new)
        l = alpha * l + jnp.sum(p, axis=-1, keepdims=True)
        acc = alpha * acc + _dot(p.astype(BF16), v)
        return m_new, l, acc

    def pair_body(jj, carries):
        out = []
        for hh in range(ATTN_HEADS_PER_STEP):
            s0 = scores(2 * jj, hh)
            s1 = scores(2 * jj + 1, hh)
            out.append(update(2 * jj + 1, s1, update(2 * jj, s0, carries[hh], hh), hh))
        return tuple(out)

    init = (jnp.full((tq, 1), -jnp.inf, F32), jnp.zeros((tq, 1), F32), jnp.zeros((tq, V_HEAD_DIM), F32))
    pairs = lax.shift_right_logical(i, 1)
    carries = lax.fori_loop(0, pairs, pair_body, (init,) * ATTN_HEADS_PER_STEP)

    def finish(with_unmasked_block):
        for hh in range(ATTN_HEADS_PER_STEP):
            carry = carries[hh]
            if with_unmasked_block:
                s_prev = scores(i - 1, hh)
            s = scores(i, hh)
            row = lax.broadcasted_iota(jnp.int32, s.shape, 0)
            col = lax.broadcasted_iota(jnp.int32, s.shape, 1)
            s = jnp.where(row >= col, s, -jnp.inf)
            if with_unmasked_block:
                carry = update(i - 1, s_prev, carry, hh)
            _, l, acc = update(i, s, carry, hh)
            o_ref[:, hh * V_HEAD_DIM:(hh + 1) * V_HEAD_DIM] = acc / l

    odd = lax.bitwise_and(i, 1)
    pl.when(odd == 1)(lambda: finish(True))
    pl.when(odd == 0)(lambda: finish(False))


def _attention(q, kv, kpe, batch, seq):
    tq = 512
    nq = seq // tq
    hps = ATTN_HEADS_PER_STEP
    kern = functools.partial(_attn_kernel, tq=tq)
    return pl.pallas_call(
        kern,
        grid=(batch, MLA_HEADS // hps, nq),
        in_specs=[
            pl.BlockSpec((tq, hps * Q_HEAD_PAD), lambda b, h, i: (b * nq + i, h)),
            pl.BlockSpec((seq, hps * (QK_NOPE_DIM + V_HEAD_DIM)), lambda b, h, i: (b, h)),
            pl.BlockSpec((seq, LANES), lambda b, h, i: (b, 0)),
        ],
        out_specs=pl.BlockSpec((tq, hps * V_HEAD_DIM), lambda b, h, i: (b * nq + i, h)),
        out_shape=jax.ShapeDtypeStruct((batch * seq, MLA_HEADS * V_HEAD_DIM), F32),
        scratch_shapes=[pltpu.VMEM((hps, seq, Q_HEAD_PAD), BF16)],
        compiler_params=_params("parallel", "parallel", "arbitrary"),
        name="mla_attention",
    )(q, kv, kpe)


def _gmlp_kernel(z_ref, gs_ref, w_ref, bt_ref, beta_ref, o_ref, y_ref):
    rows = z_ref.shape[0]
    width = GMLP_HEADS * GMLP_HEAD_DIM
    u = _gelu(z_ref[:, :width])
    v = _gelu(z_ref[:, width:])
    mu = jnp.mean(v, axis=-1, keepdims=True)
    var = jnp.mean(jnp.square(v - mu), axis=-1, keepdims=True)
    vn = ((v - mu) * lax.rsqrt(var + NORM_EPS) * gs_ref[...]).astype(BF16)
    t_idx = lax.broadcasted_iota(jnp.int32, (GMLP_CHUNK, GMLP_CHUNK), 0)
    s_idx = lax.broadcasted_iota(jnp.int32, (GMLP_CHUNK, GMLP_CHUNK), 1)
    causal = (t_idx >= s_idx).astype(F32)
    for h in range(GMLP_HEADS):
        w = (w_ref[h] * causal).astype(BF16)
        bias = bt_ref[:, h:h + 1]
        cols = slice(h * GMLP_HEAD_DIM, (h + 1) * GMLP_HEAD_DIM)
        for c in range(rows // GMLP_CHUNK):
            rs = slice(c * GMLP_CHUNK, (c + 1) * GMLP_CHUNK)
            sv = _dot(w, vn[rs, cols]) + bias
            y_ref[rs, cols] = u[rs, cols] * sv
    o_ref[...] = _rms(y_ref[...], beta_ref[...]).astype(o_ref.dtype)


def _gmlp(z_g, g_sgu, w_sgu, b_sgu, beta):
    t, two_w = z_g.shape
    width = two_w // 2
    tm = 256
    return pl.pallas_call(
        _gmlp_kernel,
        grid=(t // tm,),
        in_specs=[
            pl.BlockSpec((tm, two_w), lambda i: (i, 0)),
            pl.BlockSpec((1, width), lambda i: (0, 0)),
            pl.BlockSpec((GMLP_HEADS, GMLP_CHUNK, GMLP_CHUNK), lambda i: (0, 0, 0)),
            pl.BlockSpec((GMLP_CHUNK, GMLP_HEADS), lambda i: (0, 0)),
            pl.BlockSpec((1, width), lambda i: (0, 0)),
        ],
        out_specs=pl.BlockSpec((tm, width), lambda i: (i, 0)),
        out_shape=jax.ShapeDtypeStruct((t, width), BF16),
        scratch_shapes=[pltpu.VMEM((tm, width), F32)],
        compiler_params=_params("parallel"),
        name="gmlp_sgu",
    )(z_g, g_sgu.reshape(1, width), w_sgu, jnp.transpose(b_sgu), beta.reshape(1, width))


ROUTE_TOKENS = 256
SCATTER_GROUP = 16
STAGE_PITCH = 136


def _top_k_rows(x, k, payload=None):
    n = x.shape[0]
    row = lax.broadcasted_iota(jnp.int32, x.shape, 0).astype(F32)
    vals, picks = [], []
    for _ in range(k):
        m = jnp.max(x, axis=0, keepdims=True)
        first = jnp.min(jnp.where(x == m, row, float(n)), axis=0, keepdims=True)
        sel = row == first
        vals.append(m)
        picks.append(first if payload is None else jnp.max(jnp.where(sel, payload, -1.0), axis=0, keepdims=True))
        x = jnp.where(sel, -jnp.inf, x)
    return jnp.concatenate(vals, axis=0), jnp.concatenate(picks, axis=0)


def _pair_candidates(v0, v1, i0, i1):
    k = PEER_TOPK
    tile = 8
    b_idx = lax.broadcasted_iota(jnp.int32, (tile, v0.shape[1]), 0)
    sums = [v0[0:1] + v1]
    ids = [i0[0:1] * float(PEER_NKEYS) + i1]
    for a in range(1, tile):
        sums.append(jnp.where(b_idx < k // (a + 1), v0[a:a + 1] + v1[0:tile], -jnp.inf))
        ids.append(i0[a:a + 1] * float(PEER_NKEYS) + i1[0:tile])
    sums.append(v0[tile:k] + v1[0:1])
    ids.append(i0[tile:k] * float(PEER_NKEYS) + i1[0:1])
    return jnp.concatenate(sums, axis=0), jnp.concatenate(ids, axis=0)


def _route_kernel(q_ref, keys_ref, w_ref, gate_ref, row_ref, col_ref, *stage_refs):
    gates, experts = [], []
    for h in range(PEER_HEADS):
        sub_v, sub_i = [], []
        for p in range(2):
            q_hp = q_ref[:, (2 * h + p) * PEER_HALF:(2 * h + p + 1) * PEER_HALF]
            s = _dot_nt(keys_ref[h, p], q_hp)
            v, idx = _top_k_rows(s, PEER_TOPK)
            sub_v.append(v)
            sub_i.append(idx)
        cand, cidx = _pair_candidates(sub_v[0], sub_v[1], sub_i[0], sub_i[1])
        top_s, eidx = _top_k_rows(cand, PEER_TOPK, payload=cidx)
        e = jnp.exp(top_s - top_s[0:1])
        gates.append(e / jnp.sum(e, axis=0, keepdims=True))
        experts.append(eidx)
    gate_t = jnp.concatenate(gates, axis=0).T
    expert_t = jnp.concatenate(experts, axis=0).T.astype(jnp.int32)
    gate_ref[...] = gate_t
    row_ref[...] = lax.shift_right_logical(expert_t, PEER_NKEYS.bit_length() - 1)
    col_ref[...] = lax.bitwise_and(expert_t, PEER_NKEYS - 1)

    sub = lax.broadcasted_iota(jnp.int32, (PEER_NKEYS, PEER_SLOTS), 0)

    half = SCATTER_GROUP // 2
    for gi in range(ROUTE_TOKENS // SCATTER_GROUP):
        stage_ref = stage_refs[gi % 2]
        t0 = gi * SCATTER_GROUP
        for tl in range(SCATTER_GROUP):
            g = jnp.broadcast_to(gate_ref[t0 + tl:t0 + tl + 1, :], sub.shape)
            r = jnp.broadcast_to(row_ref[t0 + tl:t0 + tl + 1, :], sub.shape)
            c = jnp.broadcast_to(col_ref[t0 + tl:t0 + tl + 1, :], sub.shape)
            a = jnp.where(r == sub, g, 0.0).astype(BF16)
            b = jnp.where(c == sub, 1.0, 0.0).astype(BF16)
            stage_ref[tl * STAGE_PITCH:tl * STAGE_PITCH + PEER_NKEYS, :] = _dot_nt(a, b)
        for i in range(PEER_NKEYS):
            lo = stage_ref[pl.ds(i, half, stride=STAGE_PITCH), :]
            hi = stage_ref[pl.ds(half * STAGE_PITCH + i, half, stride=STAGE_PITCH), :]
            w_ref[t0:t0 + SCATTER_GROUP, i * PEER_NKEYS:(i + 1) * PEER_NKEYS] = (
                jnp.concatenate([lo, hi], axis=0).astype(w_ref.dtype))


def _route(q, keys):
    t, width = q.shape
    n_exp = PEER_NKEYS * PEER_NKEYS
    tr = ROUTE_TOKENS
    return pl.pallas_call(
        _route_kernel,
        grid=(t // tr,),
        in_specs=[
            pl.BlockSpec((tr, width), lambda i: (i, 0)),
            pl.BlockSpec(keys.shape, lambda i: (0, 0, 0, 0)),
        ],
        out_specs=pl.BlockSpec((tr, n_exp), lambda i: (i, 0)),
        out_shape=jax.ShapeDtypeStruct((t, n_exp), BF16),
        scratch_shapes=[
            pltpu.VMEM((tr, PEER_SLOTS), F32),
            pltpu.VMEM((tr, PEER_SLOTS), jnp.int32),
            pltpu.VMEM((tr, PEER_SLOTS), jnp.int32),
            pltpu.VMEM((SCATTER_GROUP * STAGE_PITCH, PEER_NKEYS), F32),
            pltpu.VMEM((SCATTER_GROUP * STAGE_PITCH, PEER_NKEYS), F32),
        ],
        compiler_params=_params("parallel"),
        name="peer_route",
    )(q, keys)


EXPERT_COLS = 512


def _expert_kernel(h_ref, gate_ref, u_ref, v_ref, o_ref, p_ref):
    c = pl.program_id(1)

    @pl.when(c == 0)
    def _():
        o_ref[...] = jnp.zeros_like(o_ref)
        p_ref[1] = jnp.zeros(p_ref.shape[1:], p_ref.dtype)

    p_prev = p_ref[(c + 1) % 2]
    for n in range(o_ref.shape[1] // EXPERT_COLS):
        cols = slice(n * EXPERT_COLS, (n + 1) * EXPERT_COLS)
        o_ref[:, cols] += _dot(p_prev, v_ref[:, cols].astype(BF16))
    a = _gelu(_dot_nt(h_ref[...], u_ref[...].astype(BF16)))
    p_ref[c % 2] = (gate_ref[...].astype(F32) * a).astype(BF16)


def _experts(h, gate, expert_u, expert_v):
    t, d = h.shape
    n_exp = expert_u.shape[0]
    tb, ec = 1024, 256
    nc = n_exp // ec
    once = pl.Buffered(1)
    return pl.pallas_call(
        _expert_kernel,
        grid=(t // tb, nc + 1),
        in_specs=[
            pl.BlockSpec((tb, d), lambda i, c: (i, 0), pipeline_mode=once),
            pl.BlockSpec((tb, ec), lambda i, c: (i, jnp.minimum(c, nc - 1))),
            pl.BlockSpec((ec, d), lambda i, c: (jnp.minimum(c, nc - 1), 0)),
            pl.BlockSpec((ec, d), lambda i, c: (jnp.maximum(c - 1, 0), 0)),
        ],
        out_specs=pl.BlockSpec((tb, d), lambda i, c: (i, 0), pipeline_mode=once),
        out_shape=jax.ShapeDtypeStruct((t, d), F32),
        scratch_shapes=[pltpu.VMEM((2, tb, ec), BF16)],
        compiler_params=_params("parallel", "arbitrary"),
        name="peer_experts",
    )(h, gate, expert_u, expert_v)


def kernel(x, c, positions, w_ada, b_ada, g_norm_mix, w_in, g_q, w_uq, g_kv, w_ukv, g_sgu, w_sgu, b_sgu, beta_mla, beta_gmlp, w_out, g_norm_ffn, w_pq, peer_keys, expert_u, expert_v, w_ada_f, b_ada_f, g_norm_f):
    batch, seq, d = x.shape
    t = batch * seq
    assert w_ada.shape[0] == 1, "single-layer trunk"
    xf = x.reshape(t, d)

    c_pad = jnp.pad(c, ((0, 16 - batch), (0, 0)))
    mod = _ada(c_pad, w_ada[0], b_ada[0])[:batch]
    sh_a, sc_a, gt_a, sh_f, sc_f, gt_f = jnp.split(mod, 6, axis=-1)
    mod_f = _ada(c_pad, w_ada_f, b_ada_f)[:batch]
    sh_o, sc_o = jnp.split(mod_f, 2, axis=-1)

    lat = Q_LORA_RANK + KV_LORA_RANK + QK_ROPE_DIM
    w_in_a = jnp.pad(w_in[0][:, :lat], ((0, 0), (0, LANES - QK_ROPE_DIM))).astype(BF16)
    w_in_g = w_in[0][:, lat:].astype(BF16)
    w_uq_pad = jnp.pad(w_uq[0].reshape(Q_LORA_RANK, MLA_HEADS, QK_HEAD_DIM),
                       ((0, 0), (0, 0), (0, Q_HEAD_PAD - QK_HEAD_DIM))).reshape(Q_LORA_RANK, MLA_HEADS * Q_HEAD_PAD).astype(BF16)

    h1 = _rms_mod(xf, g_norm_mix[0], sh_a, sc_a, seq, BF16)
    z_a = _matmul(h1, w_in_a, 512, w_in_a.shape[1], F32, "in_proj_latent")
    z_g = _matmul(h1, w_in_g, 1024, 1024, F32, "in_proj_gmlp")
    qn, kvn, kpe, cos_t, sin_t = _latent_prep(z_a, positions.reshape(t, 1), g_q[0], g_kv[0])
    q = _q_proj(qn, w_uq_pad, cos_t, sin_t)
    kv = _matmul(kvn, w_ukv[0], 1024, 1024, BF16, "kv_proj")
    y_mla = _attention(q, kv, kpe, batch, seq)
    y1 = _rms_rows(y_mla, beta_mla[0], BF16)
    y2 = _gmlp(z_g, g_sgu[0], w_sgu[0], b_sgu[0], beta_gmlp[0])
    x2 = _out_proj(y1, y2, w_out[0], xf, gt_a, seq)

    h2 = _rms_mod(x2, g_norm_ffn[0], sh_f, sc_f, seq, BF16)
    pq = _matmul(h2, w_pq[0], 1024, 512, BF16, "peer_query")
    gate = _route(pq, peer_keys[0].astype(BF16))
    f = _experts(h2, gate, expert_u[0], expert_v[0])
    out = _final(x2, f, gt_f, g_norm_f, sh_o, sc_o, seq)
    return out.reshape(batch, seq, d)
```

```python
import functools
import math

import jax
import jax.numpy as jnp
from jax import lax
from jax.experimental import pallas as pl
from jax.experimental.pallas import tpu as pltpu

F32 = jnp.float32
BF16 = jnp.bfloat16

LANES = 128
MLA_HEADS = 16
QK_NOPE_DIM = 128
QK_ROPE_DIM = 64
QK_HEAD_DIM = QK_NOPE_DIM + QK_ROPE_DIM
Q_HEAD_PAD = 2 * LANES
V_HEAD_DIM = 128
Q_LORA_RANK = 1024
KV_LORA_RANK = 512
ROPE_BASE = 10000.0
GMLP_HEADS = 16
GMLP_CHUNK = 128
GMLP_HEAD_DIM = 128
PEER_HEADS = 8
PEER_NKEYS = 128
PEER_HALF = 128
PEER_TOPK = 16
PEER_SLOTS = PEER_HEADS * PEER_TOPK
NORM_EPS = 1e-6
VMEM_LIMIT = 56 * 1024 * 1024

_NT = (((1,), (1,)), ((), ()))


def _dot(a, b):
    return jnp.dot(a, b, preferred_element_type=F32)


def _dot_nt(a, b):
    return lax.dot_general(a, b, _NT, preferred_element_type=F32)


def _params(*sem):
    return pltpu.CompilerParams(dimension_semantics=sem, vmem_limit_bytes=VMEM_LIMIT)


def _rms(x, g):
    return x * lax.rsqrt(jnp.mean(x * x, axis=-1, keepdims=True) + NORM_EPS) * g


def _gelu(x):
    return 0.5 * x * (1.0 + lax.erf(x * math.sqrt(0.5)))


def _ada_kernel(c_ref, w_ref, b_ref, o_ref):
    c = c_ref[...]
    c_act = (c * jax.nn.sigmoid(c)).astype(BF16)
    o_ref[...] = _dot(c_act, w_ref[...].astype(BF16)) + b_ref[...]


def _ada(c_pad, w, b):
    rows, d = c_pad.shape
    n = w.shape[1]
    tn = 512
    return pl.pallas_call(
        _ada_kernel,
        grid=(n // tn,),
        in_specs=[
            pl.BlockSpec((rows, d), lambda j: (0, 0)),
            pl.BlockSpec((d, tn), lambda j: (0, j)),
            pl.BlockSpec((1, tn), lambda j: (0, j)),
        ],
        out_specs=pl.BlockSpec((rows, tn), lambda j: (0, j)),
        out_shape=jax.ShapeDtypeStruct((rows, n), F32),
        compiler_params=_params("parallel"),
        name="ada",
    )(c_pad, w, b.reshape(1, n))


def _rms_mod_kernel(x_ref, g_ref, sh_ref, sc_ref, o_ref):
    y = _rms(x_ref[...], g_ref[...])
    o_ref[...] = (y * (1.0 + sc_ref[...]) + sh_ref[...]).astype(o_ref.dtype)


def _rms_mod(x, g, shift, scale, seq, out_dtype):
    t, d = x.shape
    tm = 256
    per_batch = seq // tm
    mod_spec = pl.BlockSpec((None, 1, d), lambda i: (i // per_batch, 0, 0))
    return pl.pallas_call(
        _rms_mod_kernel,
        grid=(t // tm,),
        in_specs=[
            pl.BlockSpec((tm, d), lambda i: (i, 0)),
            pl.BlockSpec((1, d), lambda i: (0, 0)),
            mod_spec,
            mod_spec,
        ],
        out_specs=pl.BlockSpec((tm, d), lambda i: (i, 0)),
        out_shape=jax.ShapeDtypeStruct((t, d), out_dtype),
        compiler_params=_params("parallel"),
        name="rms_mod",
    )(x, g.reshape(1, d), shift[:, None, :], scale[:, None, :])


def _rms_kernel(x_ref, g_ref, o_ref):
    o_ref[...] = _rms(x_ref[...], g_ref[...]).astype(o_ref.dtype)


def _rms_rows(x, g, out_dtype):
    t, d = x.shape
    tm = 512
    return pl.pallas_call(
        _rms_kernel,
        grid=(t // tm,),
        in_specs=[pl.BlockSpec((tm, d), lambda i: (i, 0)), pl.BlockSpec((1, d), lambda i: (0, 0))],
        out_specs=pl.BlockSpec((tm, d), lambda i: (i, 0)),
        out_shape=jax.ShapeDtypeStruct((t, d), out_dtype),
        compiler_params=_params("parallel"),
        name="rms_rows",
    )(x, g.reshape(1, d))


def _final_kernel(x_ref, f_ref, gt_ref, g_ref, sh_ref, sc_ref, o_ref):
    x = x_ref[...] + gt_ref[...] * f_ref[...]
    o_ref[...] = _rms(x, g_ref[...]) * (1.0 + sc_ref[...]) + sh_ref[...]


def _final(x, f, gate, g, shift, scale, seq):
    t, d = x.shape
    tm = 256
    per_batch = seq // tm
    row = pl.BlockSpec((tm, d), lambda i: (i, 0))
    mod_spec = pl.BlockSpec((None, 1, d), lambda i: (i // per_batch, 0, 0))
    return pl.pallas_call(
        _final_kernel,
        grid=(t // tm,),
        in_specs=[row, row, mod_spec, pl.BlockSpec((1, d), lambda i: (0, 0)), mod_spec, mod_spec],
        out_specs=row,
        out_shape=jax.ShapeDtypeStruct((t, d), F32),
        compiler_params=_params("parallel"),
        name="final_norm",
    )(x, f, gate[:, None, :], g.reshape(1, d), shift[:, None, :], scale[:, None, :])


def _mm_kernel(a_ref, w_ref, o_ref):
    o_ref[...] = _dot(a_ref[...], w_ref[...].astype(BF16)).astype(o_ref.dtype)


def _matmul(a, w, tm, tn, out_dtype, name):
    m, k = a.shape
    n = w.shape[1]
    return pl.pallas_call(
        _mm_kernel,
        grid=(m // tm, n // tn),
        in_specs=[pl.BlockSpec((tm, k), lambda i, j: (i, 0)), pl.BlockSpec((k, tn), lambda i, j: (0, j))],
        out_specs=pl.BlockSpec((tm, tn), lambda i, j: (i, j)),
        out_shape=jax.ShapeDtypeStruct((m, n), out_dtype),
        compiler_params=_params("parallel", "parallel"),
        name=name,
    )(a, w)


def _swap_rope_halves(x):
    lane = lax.broadcasted_iota(jnp.int32, x.shape, 1)
    half = QK_ROPE_DIM // 2
    return jnp.where(lane < half, pltpu.roll(x, LANES - half, 1), pltpu.roll(x, half, 1))


def _rope_tile(x, cos_t, sin_t):
    return x * cos_t + _swap_rope_halves(x) * sin_t


def _q_proj_kernel(a_ref, w_ref, cos_ref, sin_ref, o_ref, *, heads_per_block, scale):
    acc = _dot(a_ref[...], w_ref[...])
    cos_t = cos_ref[...]
    sin_t = sin_ref[...]
    for h in range(heads_per_block):
        base = h * Q_HEAD_PAD
        nope = acc[:, base:base + LANES]
        pe = _rope_tile(acc[:, base + LANES:base + Q_HEAD_PAD], cos_t, sin_t)
        o_ref[:, base:base + LANES] = (nope * scale).astype(o_ref.dtype)
        o_ref[:, base + LANES:base + Q_HEAD_PAD] = (pe * scale).astype(o_ref.dtype)


def _q_proj(qn, w_uq_pad, cos_t, sin_t):
    m, k = qn.shape
    n = w_uq_pad.shape[1]
    tm, tn = 1024, 1024
    kern = functools.partial(_q_proj_kernel, heads_per_block=tn // Q_HEAD_PAD, scale=1.0 / math.sqrt(QK_HEAD_DIM))
    return pl.pallas_call(
        kern,
        grid=(m // tm, n // tn),
        in_specs=[
            pl.BlockSpec((tm, k), lambda i, j: (i, 0)),
            pl.BlockSpec((k, tn), lambda i, j: (0, j)),
            pl.BlockSpec((tm, LANES), lambda i, j: (i, 0)),
            pl.BlockSpec((tm, LANES), lambda i, j: (i, 0)),
        ],
        out_specs=pl.BlockSpec((tm, tn), lambda i, j: (i, j)),
        out_shape=jax.ShapeDtypeStruct((m, n), BF16),
        compiler_params=_params("parallel", "parallel"),
        name="q_proj",
    )(qn, w_uq_pad, cos_t, sin_t)


def _out_proj_kernel(a1_ref, a2_ref, w_ref, x_ref, gt_ref, o_ref):
    k1 = a1_ref.shape[1]
    w = w_ref[...].astype(BF16)
    acc = _dot(a1_ref[...], w[:k1]) + _dot(a2_ref[...], w[k1:])
    o_ref[...] = x_ref[...] + gt_ref[...] * acc


def _out_proj(a1, a2, w, x, gate, seq):
    m, k1 = a1.shape
    k2 = a2.shape[1]
    n = w.shape[1]
    tm, tn = 1024, 512
    per_batch = seq // tm
    return pl.pallas_call(
        _out_proj_kernel,
        grid=(m // tm, n // tn),
        in_specs=[
            pl.BlockSpec((tm, k1), lambda i, j: (i, 0)),
            pl.BlockSpec((tm, k2), lambda i, j: (i, 0)),
            pl.BlockSpec((k1 + k2, tn), lambda i, j: (0, j)),
            pl.BlockSpec((tm, tn), lambda i, j: (i, j)),
            pl.BlockSpec((None, 1, tn), lambda i, j: (i // per_batch, 0, j)),
        ],
        out_specs=pl.BlockSpec((tm, tn), lambda i, j: (i, j)),
        out_shape=jax.ShapeDtypeStruct((m, n), F32),
        compiler_params=_params("parallel", "parallel"),
        name="out_proj",
    )(a1, a2, w, x, gate[:, None, :])


def _latent_prep_kernel(z_ref, pos_ref, gq_ref, gkv_ref, freq_ref, cmask_ref, smask_ref, wkv_ref,
                        qn_ref, kv_ref, kpe_ref, cos_ref, sin_ref):
    q_lat = z_ref[:, :Q_LORA_RANK]
    kv_lat = z_ref[:, Q_LORA_RANK:Q_LORA_RANK + KV_LORA_RANK]
    k_pe = z_ref[:, Q_LORA_RANK + KV_LORA_RANK:]
    qn_ref[...] = _rms(q_lat, gq_ref[...]).astype(qn_ref.dtype)
    kv_ref[...] = _dot(_rms(kv_lat, gkv_ref[...]).astype(BF16), wkv_ref[...]).astype(kv_ref.dtype)
    ang = pos_ref[...].astype(F32) * freq_ref[...]
    cos_t = jnp.cos(ang) * cmask_ref[...]
    sin_t = jnp.sin(ang) * smask_ref[...]
    cos_ref[...] = cos_t
    sin_ref[...] = sin_t
    kpe_ref[...] = _rope_tile(k_pe, cos_t, sin_t).astype(kpe_ref.dtype)


def _latent_prep(z_a, pos, g_q, g_kv, w_ukv):
    t, width = z_a.shape
    kv_width = w_ukv.shape[1]
    tm = 512
    half = QK_ROPE_DIM // 2
    inv_freq = ROPE_BASE ** (-jnp.arange(0, QK_ROPE_DIM, 2, dtype=F32) / QK_ROPE_DIM)
    zeros = jnp.zeros((LANES - QK_ROPE_DIM,), F32)
    freq = jnp.concatenate([inv_freq, inv_freq, zeros]).reshape(1, LANES)
    cmask = jnp.concatenate([jnp.ones((QK_ROPE_DIM,), F32), zeros]).reshape(1, LANES)
    smask = jnp.concatenate([-jnp.ones((half,), F32), jnp.ones((half,), F32), zeros]).reshape(1, LANES)
    const = lambda n: pl.BlockSpec((1, n), lambda i: (0, 0))
    rows = lambda n: pl.BlockSpec((tm, n), lambda i: (i, 0))
    return pl.pallas_call(
        _latent_prep_kernel,
        grid=(t // tm,),
        in_specs=[rows(width), rows(1), const(Q_LORA_RANK), const(KV_LORA_RANK),
                  const(LANES), const(LANES), const(LANES),
                  pl.BlockSpec((KV_LORA_RANK, kv_width), lambda i: (0, 0))],
        out_specs=[rows(Q_LORA_RANK), rows(kv_width), rows(LANES), rows(LANES), rows(LANES)],
        out_shape=[
            jax.ShapeDtypeStruct((t, Q_LORA_RANK), BF16),
            jax.ShapeDtypeStruct((t, kv_width), BF16),
            jax.ShapeDtypeStruct((t, LANES), BF16),
            jax.ShapeDtypeStruct((t, LANES), F32),
            jax.ShapeDtypeStruct((t, LANES), F32),
        ],
        compiler_params=_params("parallel"),
        name="latent_prep",
    )(z_a, pos, g_q.reshape(1, -1), g_kv.reshape(1, -1), freq, cmask, smask, w_ukv)


ATTN_HEADS_PER_STEP = 2


def _attn_kernel(q_ref, kv_ref, kpe_ref, o_ref, k_ref, *, tq):
    i = pl.program_id(2)
    kv_w = QK_NOPE_DIM + V_HEAD_DIM

    @pl.when(i == 0)
    def _():
        for hh in range(ATTN_HEADS_PER_STEP):
            k_ref[hh, :, :QK_NOPE_DIM] = kv_ref[:, hh * kv_w:hh * kv_w + QK_NOPE_DIM]
            k_ref[hh, :, QK_NOPE_DIM:] = kpe_ref[...]

    qs = [q_ref[:, hh * Q_HEAD_PAD:(hh + 1) * Q_HEAD_PAD] for hh in range(ATTN_HEADS_PER_STEP)]

    def scores(j, hh):
        off = pl.multiple_of(j * tq, tq)
        return _dot_nt(qs[hh], k_ref[hh, pl.ds(off, tq), :])

    def update(j, s, carry, hh):
        off = pl.multiple_of(j * tq, tq)
        v = kv_ref[pl.ds(off, tq), hh * kv_w + QK_NOPE_DIM:(hh + 1) * kv_w]
        m, l, acc = carry
        m_new = jnp.maximum(m, jnp.max(s, axis=-1, keepdims=True))
        alpha = jnp.exp(m - m_new)
        p = jnp.exp(s - m_new)
        l = alpha * l + jnp.sum(p, axis=-1, keepdims=True)
        acc = alpha * acc + _dot(p.astype(BF16), v)
        return m_new, l, acc

    def pair_body(jj, carries):
        out = []
        for hh in range(ATTN_HEADS_PER_STEP):
            s0 = scores(2 * jj, hh)
            s1 = scores(2 * jj + 1, hh)
            out.append(update(2 * jj + 1, s1, update(2 * jj, s0, carries[hh], hh), hh))
        return tuple(out)

    init = (jnp.full((tq, 1), -jnp.inf, F32), jnp.zeros((tq, 1), F32), jnp.zeros((tq, V_HEAD_DIM), F32))
    pairs = lax.shift_right_logical(i, 1)
    carries = lax.fori_loop(0, pairs, pair_body, (init,) * ATTN_HEADS_PER_STEP)

    def finish(with_unmasked_block):
        for hh in range(ATTN_HEADS_PER_STEP):
            carry = carries[hh]
            if with_unmasked_block:
                s_prev = scores(i - 1, hh)
            s = scores(i, hh)
            row = lax.broadcasted_iota(jnp.int32, s.shape, 0)
            col = lax.broadcasted_iota(jnp.int32, s.shape, 1)
            s = jnp.where(row >= col, s, -jnp.inf)
            if with_unmasked_block:
                carry = update(i - 1, s_prev, carry, hh)
            _, l, acc = update(i, s, carry, hh)
            o_ref[:, hh * V_HEAD_DIM:(hh + 1) * V_HEAD_DIM] = acc / l

    odd = lax.bitwise_and(i, 1)
    pl.when(odd == 1)(lambda: finish(True))
    pl.when(odd == 0)(lambda: finish(False))


def _attention(q, kv, kpe, batch, seq):
    tq = 512
    nq = seq // tq
    hps = ATTN_HEADS_PER_STEP
    kern = functools.partial(_attn_kernel, tq=tq)
    return pl.pallas_call(
        kern,
        grid=(batch, MLA_HEADS // hps, nq),
        in_specs=[
            pl.BlockSpec((tq, hps * Q_HEAD_PAD), lambda b, h, i: (b * nq + i, h)),
            pl.BlockSpec((seq, hps * (QK_NOPE_DIM + V_HEAD_DIM)), lambda b, h, i: (b, h)),
            pl.BlockSpec((seq, LANES), lambda b, h, i: (b, 0)),
        ],
        out_specs=pl.BlockSpec((tq, hps * V_HEAD_DIM), lambda b, h, i: (b * nq + i, h)),
        out_shape=jax.ShapeDtypeStruct((batch * seq, MLA_HEADS * V_HEAD_DIM), F32),
        scratch_shapes=[pltpu.VMEM((hps, seq, Q_HEAD_PAD), BF16)],
        compiler_params=_params("parallel", "parallel", "arbitrary"),
        name="mla_attention",
    )(q, kv, kpe)


def _gmlp_kernel(z_ref, gs_ref, w_ref, bt_ref, beta_ref, o_ref, y_ref):
    rows = z_ref.shape[0]
    width = GMLP_HEADS * GMLP_HEAD_DIM
    u = _gelu(z_ref[:, :width])
    v = _gelu(z_ref[:, width:])
    mu = jnp.mean(v, axis=-1, keepdims=True)
    var = jnp.mean(jnp.square(v - mu), axis=-1, keepdims=True)
    vn = ((v - mu) * lax.rsqrt(var + NORM_EPS) * gs_ref[...]).astype(BF16)
    t_idx = lax.broadcasted_iota(jnp.int32, (GMLP_CHUNK, GMLP_CHUNK), 0)
    s_idx = lax.broadcasted_iota(jnp.int32, (GMLP_CHUNK, GMLP_CHUNK), 1)
    causal = (t_idx >= s_idx).astype(F32)
    for h in range(GMLP_HEADS):
        w = (w_ref[h] * causal).astype(BF16)
        bias = bt_ref[:, h:h + 1]
        cols = slice(h * GMLP_HEAD_DIM, (h + 1) * GMLP_HEAD_DIM)
        for c in range(rows // GMLP_CHUNK):
            rs = slice(c * GMLP_CHUNK, (c + 1) * GMLP_CHUNK)
            sv = _dot(w, vn[rs, cols]) + bias
            y_ref[rs, cols] = u[rs, cols] * sv
    o_ref[...] = _rms(y_ref[...], beta_ref[...]).astype(o_ref.dtype)


def _gmlp(z_g, g_sgu, w_sgu, b_sgu, beta):
    t, two_w = z_g.shape
    width = two_w // 2
    tm = 256
    return pl.pallas_call(
        _gmlp_kernel,
        grid=(t // tm,),
        in_specs=[
            pl.BlockSpec((tm, two_w), lambda i: (i, 0)),
            pl.BlockSpec((1, width), lambda i: (0, 0)),
            pl.BlockSpec((GMLP_HEADS, GMLP_CHUNK, GMLP_CHUNK), lambda i: (0, 0, 0)),
            pl.BlockSpec((GMLP_CHUNK, GMLP_HEADS), lambda i: (0, 0)),
            pl.BlockSpec((1, width), lambda i: (0, 0)),
        ],
        out_specs=pl.BlockSpec((tm, width), lambda i: (i, 0)),
        out_shape=jax.ShapeDtypeStruct((t, width), BF16),
        scratch_shapes=[pltpu.VMEM((tm, width), F32)],
        compiler_params=_params("parallel"),
        name="gmlp_sgu",
    )(z_g, g_sgu.reshape(1, width), w_sgu, jnp.transpose(b_sgu), beta.reshape(1, width))


ROUTE_TOKENS = 256
SCATTER_GROUP = 16
STAGE_PITCH = 136


def _top_k_rows(x, k, payload=None):
    n = x.shape[0]
    row = lax.broadcasted_iota(jnp.int32, x.shape, 0).astype(F32)
    vals, picks = [], []
    for _ in range(k):
        m = jnp.max(x, axis=0, keepdims=True)
        first = jnp.min(jnp.where(x == m, row, float(n)), axis=0, keepdims=True)
        sel = row == first
        vals.append(m)
        picks.append(first if payload is None else jnp.max(jnp.where(sel, payload, -1.0), axis=0, keepdims=True))
        x = jnp.where(sel, -jnp.inf, x)
    return jnp.concatenate(vals, axis=0), jnp.concatenate(picks, axis=0)


def _pair_candidates(v0, v1, i0, i1):
    k = PEER_TOPK
    tile = 8
    b_idx = lax.broadcasted_iota(jnp.int32, (tile, v0.shape[1]), 0)
    sums = [v0[0:1] + v1]
    ids = [i0[0:1] * float(PEER_NKEYS) + i1]
    for a in range(1, tile):
        sums.append(jnp.where(b_idx < k // (a + 1), v0[a:a + 1] + v1[0:tile], -jnp.inf))
        ids.append(i0[a:a + 1] * float(PEER_NKEYS) + i1[0:tile])
    sums.append(v0[tile:k] + v1[0:1])
    ids.append(i0[tile:k] * float(PEER_NKEYS) + i1[0:1])
    return jnp.concatenate(sums, axis=0), jnp.concatenate(ids, axis=0)


def _route_kernel(q_ref, keys_ref, w_ref, gate_ref, row_ref, col_ref, *stage_refs):
    gates, experts = [], []
    for h in range(PEER_HEADS):
        sub_v, sub_i = [], []
        for p in range(2):
            q_hp = q_ref[:, (2 * h + p) * PEER_HALF:(2 * h + p + 1) * PEER_HALF]
            s = _dot_nt(keys_ref[h, p], q_hp)
            v, idx = _top_k_rows(s, PEER_TOPK)
            sub_v.append(v)
            sub_i.append(idx)
        cand, cidx = _pair_candidates(sub_v[0], sub_v[1], sub_i[0], sub_i[1])
        top_s, eidx = _top_k_rows(cand, PEER_TOPK, payload=cidx)
        e = jnp.exp(top_s - top_s[0:1])
        gates.append(e / jnp.sum(e, axis=0, keepdims=True))
        experts.append(eidx)
    gate_t = jnp.concatenate(gates, axis=0).T
    expert_t = jnp.concatenate(experts, axis=0).T.astype(jnp.int32)
    gate_ref[...] = gate_t
    row_ref[...] = lax.shift_right_logical(expert_t, PEER_NKEYS.bit_length() - 1)
    col_ref[...] = lax.bitwise_and(expert_t, PEER_NKEYS - 1)

    sub = lax.broadcasted_iota(jnp.int32, (PEER_NKEYS, PEER_SLOTS), 0)

    half = SCATTER_GROUP // 2
    for gi in range(ROUTE_TOKENS // SCATTER_GROUP):
        stage_ref = stage_refs[gi % 2]
        t0 = gi * SCATTER_GROUP
        for tl in range(SCATTER_GROUP):
            g = jnp.broadcast_to(gate_ref[t0 + tl:t0 + tl + 1, :], sub.shape)
            r = jnp.broadcast_to(row_ref[t0 + tl:t0 + tl + 1, :], sub.shape)
            c = jnp.broadcast_to(col_ref[t0 + tl:t0 + tl + 1, :], sub.shape)
            a = jnp.where(r == sub, g, 0.0).astype(BF16)
            b = jnp.where(c == sub, 1.0, 0.0).astype(BF16)
            stage_ref[tl * STAGE_PITCH:tl * STAGE_PITCH + PEER_NKEYS, :] = _dot_nt(a, b)
        for i in range(PEER_NKEYS):
            lo = stage_ref[pl.ds(i, half, stride=STAGE_PITCH), :]
            hi = stage_ref[pl.ds(half * STAGE_PITCH + i, half, stride=STAGE_PITCH), :]
            w_ref[t0:t0 + SCATTER_GROUP, i * PEER_NKEYS:(i + 1) * PEER_NKEYS] = (
                jnp.concatenate([lo, hi], axis=0).astype(w_ref.dtype))


def _route(q, keys):
    t, width = q.shape
    n_exp = PEER_NKEYS * PEER_NKEYS
    tr = ROUTE_TOKENS
    return pl.pallas_call(
        _route_kernel,
        grid=(t // tr,),
        in_specs=[
            pl.BlockSpec((tr, width), lambda i: (i, 0)),
            pl.BlockSpec(keys.shape, lambda i: (0, 0, 0, 0)),
        ],
        out_specs=pl.BlockSpec((tr, n_exp), lambda i: (i, 0)),
        out_shape=jax.ShapeDtypeStruct((t, n_exp), BF16),
        scratch_shapes=[
            pltpu.VMEM((tr, PEER_SLOTS), F32),
            pltpu.VMEM((tr, PEER_SLOTS), jnp.int32),
            pltpu.VMEM((tr, PEER_SLOTS), jnp.int32),
            pltpu.VMEM((SCATTER_GROUP * STAGE_PITCH, PEER_NKEYS), F32),
            pltpu.VMEM((SCATTER_GROUP * STAGE_PITCH, PEER_NKEYS), F32),
        ],
        compiler_params=_params("parallel"),
        name="peer_route",
    )(q, keys)


EXPERT_COLS = 512


def _expert_kernel(h_ref, gate_ref, u_ref, v_ref, o_ref, p_ref):
    c = pl.program_id(1)

    @pl.when(c == 0)
    def _():
        o_ref[...] = jnp.zeros_like(o_ref)
        p_ref[1] = jnp.zeros(p_ref.shape[1:], p_ref.dtype)

    p_prev = p_ref[(c + 1) % 2]
    for n in range(o_ref.shape[1] // EXPERT_COLS):
        cols = slice(n * EXPERT_COLS, (n + 1) * EXPERT_COLS)
        o_ref[:, cols] += _dot(p_prev, v_ref[:, cols].astype(BF16))
    a = _gelu(_dot_nt(h_ref[...], u_ref[...].astype(BF16)))
    p_ref[c % 2] = (gate_ref[...].astype(F32) * a).astype(BF16)


def _experts(h, gate, expert_u, expert_v):
    t, d = h.shape
    n_exp = expert_u.shape[0]
    tb, ec = 1024, 256
    nc = n_exp // ec
    once = pl.Buffered(1)
    return pl.pallas_call(
        _expert_kernel,
        grid=(t // tb, nc + 1),
        in_specs=[
            pl.BlockSpec((tb, d), lambda i, c: (i, 0), pipeline_mode=once),
            pl.BlockSpec((tb, ec), lambda i, c: (i, jnp.minimum(c, nc - 1))),
            pl.BlockSpec((ec, d), lambda i, c: (jnp.minimum(c, nc - 1), 0)),
            pl.BlockSpec((ec, d), lambda i, c: (jnp.maximum(c - 1, 0), 0)),
        ],
        out_specs=pl.BlockSpec((tb, d), lambda i, c: (i, 0), pipeline_mode=once),
        out_shape=jax.ShapeDtypeStruct((t, d), F32),
        scratch_shapes=[pltpu.VMEM((2, tb, ec), BF16)],
        compiler_params=_params("parallel", "arbitrary"),
        name="peer_experts",
    )(h, gate, expert_u, expert_v)


def kernel(x, c, positions, w_ada, b_ada, g_norm_mix, w_in, g_q, w_uq, g_kv, w_ukv, g_sgu, w_sgu, b_sgu, beta_mla, beta_gmlp, w_out, g_norm_ffn, w_pq, peer_keys, expert_u, expert_v, w_ada_f, b_ada_f, g_norm_f):
    batch, seq, d = x.shape
    t = batch * seq
    assert w_ada.shape[0] == 1, "single-layer trunk"
    xf = x.reshape(t, d)

    c_pad = jnp.pad(c, ((0, 16 - batch), (0, 0)))
    mod = _ada(c_pad, w_ada[0], b_ada[0])[:batch]
    sh_a, sc_a, gt_a, sh_f, sc_f, gt_f = jnp.split(mod, 6, axis=-1)
    mod_f = _ada(c_pad, w_ada_f, b_ada_f)[:batch]
    sh_o, sc_o = jnp.split(mod_f, 2, axis=-1)

    lat = Q_LORA_RANK + KV_LORA_RANK + QK_ROPE_DIM
    w_in_a = jnp.pad(w_in[0][:, :lat], ((0, 0), (0, LANES - QK_ROPE_DIM))).astype(BF16)
    w_in_g = w_in[0][:, lat:].astype(BF16)
    w_uq_pad = jnp.pad(w_uq[0].reshape(Q_LORA_RANK, MLA_HEADS, QK_HEAD_DIM),
                       ((0, 0), (0, 0), (0, Q_HEAD_PAD - QK_HEAD_DIM))).reshape(Q_LORA_RANK, MLA_HEADS * Q_HEAD_PAD).astype(BF16)

    h1 = _rms_mod(xf, g_norm_mix[0], sh_a, sc_a, seq, BF16)
    z_a = _matmul(h1, w_in_a, 512, w_in_a.shape[1], F32, "in_proj_latent")
    z_g = _matmul(h1, w_in_g, 1024, 1024, F32, "in_proj_gmlp")
    qn, kv, kpe, cos_t, sin_t = _latent_prep(z_a, positions.reshape(t, 1), g_q[0], g_kv[0], w_ukv[0].astype(BF16))
    q = _q_proj(qn, w_uq_pad, cos_t, sin_t)
    y_mla = _attention(q, kv, kpe, batch, seq)
    y1 = _rms_rows(y_mla, beta_mla[0], BF16)
    y2 = _gmlp(z_g, g_sgu[0], w_sgu[0], b_sgu[0], beta_gmlp[0])
    x2 = _out_proj(y1, y2, w_out[0], xf, gt_a, seq)

    h2 = _rms_mod(x2, g_norm_ffn[0], sh_f, sc_f, seq, BF16)
    pq = _matmul(h2, w_pq[0], 1024, 512, BF16, "peer_query")
    gate = _route(pq, peer_keys[0].astype(BF16))
    f = _experts(h2, gate, expert_u[0], expert_v[0])
    out = _final(x2, f, gt_f, g_norm_f, sh_o, sc_o, seq)
    return out.reshape(batch, seq, d)
```
